```python
import math
import jax
import jax.numpy as jnp
from jax import lax
import numpy as np

D_MODEL = 1024
BATCH = 8
SEQ = 4096
DEPTH = 2

D_FF = 2816
MLA_HEADS = 4
MLA_NOPE_DIM = 64
MLA_ROPE_DIM = 32
MLA_V_DIM = 64
MLA_Q_LORA = 256
MLA_KV_LORA = 128
DIFF_HEADS = 4
DIFF_QK_DIM = 32
DIFF_V_DIM = 64
NSA_HEADS = 8
NSA_GROUPS = 2
NSA_DK = 64
NSA_DV = 64
CMP_BLOCK = 32
CMP_STRIDE = 16
CMP_HIDDEN = 256
SEL_BLOCK = 64
SEL_TOPK = 16
WINDOW = 512
NSA_Q_BLOCK = 64
ATTN_Q_BLOCK = 128
ROPE_THETA = 10000.0
EPS = 1e-6
NEG = -1e30
FORCE = 1e4

D_MIX = MLA_HEADS * MLA_V_DIM + DIFF_HEADS * DIFF_V_DIM + NSA_HEADS * NSA_DV
IN_SPLITS = (
    MLA_Q_LORA,
    MLA_KV_LORA,
    MLA_ROPE_DIM,
    DIFF_HEADS * 2 * DIFF_QK_DIM,
    DIFF_HEADS * 2 * DIFF_QK_DIM,
    DIFF_HEADS * DIFF_V_DIM,
    NSA_HEADS * NSA_DK,
    3 * 2 * NSA_GROUPS * NSA_DK,
    NSA_HEADS * 3,
)
N_IN = sum(IN_SPLITS)

kernel_name = 'hymba_mla_diff_nsa_macaron'


def rms_norm(x, g):
    xf = x.astype(jnp.float32)
    y = xf * lax.rsqrt(jnp.mean(xf * xf, axis=-1, keepdims=True) + EPS)
    return (y * g.astype(jnp.float32)).astype(x.dtype)


def rope_tables(seq, dim):
    inv = ROPE_THETA ** (-jnp.arange(0, dim, 2, dtype=jnp.float32) / dim)
    ang = jnp.arange(seq, dtype=jnp.float32)[:, None] * inv[None, :]
    return jnp.cos(ang), jnp.sin(ang)


def apply_rope(x, cos, sin):
    half = x.shape[-1] // 2
    x1, x2 = x[..., :half], x[..., half:]
    c = cos[None, :, None, :].astype(x.dtype)
    s = sin[None, :, None, :].astype(x.dtype)
    return jnp.concatenate([x1 * c - x2 * s, x2 * c + x1 * s], axis=-1)


def swiglu(h, wg, wu, wd):
    return (jax.nn.silu(h @ wg) * (h @ wu)) @ wd


def to_blocks(a, nb, qb):
    return a.reshape((a.shape[0], nb, qb) + a.shape[2:]).swapaxes(0, 1)


def from_blocks(a):
    nb, b, qb = a.shape[:3]
    return a.swapaxes(0, 1).reshape((b, nb * qb) + a.shape[3:])


def mla_attention(q_nope, q_rope, k_nope, k_rope, v):
    t = q_nope.shape[1]
    nb = t // ATTN_Q_BLOCK
    kpos = jnp.arange(t)
    scale = (MLA_NOPE_DIM + MLA_ROPE_DIM) ** -0.5

    def block(args):
        qn, qr, bi = args
        qpos = bi * ATTN_Q_BLOCK + jnp.arange(ATTN_Q_BLOCK)
        s = (jnp.einsum('bqhd,bkhd->bhqk', qn, k_nope)
             + jnp.einsum('bqhr,bkr->bhqk', qr, k_rope)).astype(jnp.float32) * scale
        s = jnp.where(kpos[None, :] <= qpos[:, None], s, NEG)
        p = jax.nn.softmax(s, axis=-1).astype(v.dtype)
        return jnp.einsum('bhqk,bkhd->bqhd', p, v)

    out = lax.map(block, (to_blocks(q_nope, nb, ATTN_Q_BLOCK),
                          to_blocks(q_rope, nb, ATTN_Q_BLOCK), jnp.arange(nb)))
    return from_blocks(out)


def diff_attention(q, k, v, lam):
    t = q.shape[1]
    nb = t // ATTN_Q_BLOCK
    kpos = jnp.arange(t)
    scale = DIFF_QK_DIM ** -0.5

    def block(args):
        qb, bi = args
        qpos = bi * ATTN_Q_BLOCK + jnp.arange(ATTN_Q_BLOCK)
        s = jnp.einsum('bqhmd,bkhmd->bhmqk', qb, k).astype(jnp.float32) * scale
        s = jnp.where(kpos[None, :] <= qpos[:, None], s, NEG)
        p = jax.nn.softmax(s, axis=-1)
        a = p[:, :, 0] - lam * p[:, :, 1]
        return jnp.einsum('bhqk,bkhd->bqhd', a.astype(v.dtype), v)

    out = lax.map(block, (to_blocks(q, nb, ATTN_Q_BLOCK), jnp.arange(nb)))
    return from_blocks(out)


def compress_tokens(tok, pos_emb, w1, b1, w2):
    b, t, g, d = tok.shape
    nc = (t - CMP_BLOCK) // CMP_STRIDE + 1
    idx = np.arange(nc)[:, None] * CMP_STRIDE + np.arange(CMP_BLOCK)[None, :]
    blk = tok[:, idx] + pos_emb[None, None, :, None, :]
    flat = blk.transpose(0, 1, 3, 2, 4).reshape(b, nc, g, CMP_BLOCK * d)
    return jax.nn.silu(flat @ w1 + b1) @ w2


def nsa_attention(q, k_cmp, v_cmp, k_slc, v_slc, k_win, v_win, gates):
    b, t, h, dk = q.shape
    g = NSA_GROUPS
    hg = h // g
    dv = v_slc.shape[-1]
    nc = k_cmp.shape[1]
    nb_sel = t // SEL_BLOCK
    n_sel = min(SEL_TOPK, nb_sel)
    qn = NSA_Q_BLOCK
    nq = t // qn
    scale = dk ** -0.5
    cmp_end = jnp.arange(nc) * CMP_STRIDE + CMP_BLOCK - 1
    r_sel = SEL_BLOCK // CMP_STRIDE
    r_cmp = CMP_BLOCK // CMP_STRIDE
    map_idx = (np.arange(nb_sel)[:, None, None] * r_sel
               - np.arange(r_sel)[None, :, None]
               - np.arange(r_cmp)[None, None, :]).reshape(nb_sel, -1)
    map_valid = (map_idx >= 0) & (map_idx < nc)
    map_idx = np.clip(map_idx, 0, nc - 1)
    k_blk = k_slc.reshape(b, nb_sel, SEL_BLOCK, g, dk).transpose(0, 3, 1, 2, 4)
    v_blk = v_slc.reshape(b, nb_sel, SEL_BLOCK, g, dv).transpose(0, 3, 1, 2, 4)
    k_wp = jnp.pad(k_win, ((0, 0), (WINDOW, 0), (0, 0), (0, 0)))
    v_wp = jnp.pad(v_win, ((0, 0), (WINDOW, 0), (0, 0), (0, 0)))
    bi = jnp.arange(b)[:, None, None, None]
    gi = jnp.arange(g)[None, :, None, None]
    blk_ids = jnp.arange(nb_sel)
    tok_in_blk = jnp.arange(SEL_BLOCK)
    win_off = jnp.arange(qn + WINDOW)

    def chunk(args):
        qc, gc, ci = args
        qg = qc.reshape(b, qn, g, hg, dk)
        tpos = ci * qn + jnp.arange(qn)
        s = jnp.einsum('bqghd,bngd->bghqn', qg, k_cmp).astype(jnp.float32) * scale
        valid = cmp_end[None, :] <= tpos[:, None]
        p_c = jax.nn.softmax(jnp.where(valid, s, NEG), axis=-1) * valid
        o_cmp = jnp.einsum('bghqn,bngd->bqghd', p_c.astype(v_cmp.dtype), v_cmp)
        pg = p_c.sum(axis=2)
        p_slc = jnp.where(map_valid, pg[..., map_idx], 0.0).sum(-1)
        cblk = (tpos // SEL_BLOCK)[:, None]
        forced = (blk_ids == 0) | (blk_ids == cblk) | (blk_ids == cblk - 1)
        score = jnp.where(blk_ids > cblk, -1.0, p_slc + jnp.where(forced, FORCE, 0.0))
        _, sel = lax.top_k(score, n_sel)
        k_sel = k_blk[bi, gi, sel]
        v_sel = v_blk[bi, gi, sel]
        kpos = sel[..., None] * SEL_BLOCK + tok_in_blk
        m_s = (kpos <= tpos[None, None, :, None, None])[:, :, None]
        s = jnp.einsum('bqghd,bgqnkd->bghqnk', qg, k_sel).astype(jnp.float32) * scale
        p_s = jax.nn.softmax(jnp.where(m_s, s, NEG), axis=(-2, -1))
        o_slc = jnp.einsum('bghqnk,bgqnkd->bqghd', p_s.astype(v_sel.dtype), v_sel)
        k_w = lax.dynamic_slice_in_dim(k_wp, ci * qn, qn + WINDOW, axis=1)
        v_w = lax.dynamic_slice_in_dim(v_wp, ci * qn, qn + WINDOW, axis=1)
        kpos_w = ci * qn - WINDOW + win_off
        delta = tpos[:, None] - kpos_w[None, :]
        m_w = (delta >= 0) & (delta < WINDOW) & (kpos_w >= 0)[None, :]
        s = jnp.einsum('bqghd,bkgd->bghqk', qg, k_w).astype(jnp.float32) * scale
        p_w = jax.nn.softmax(jnp.where(m_w, s, NEG), axis=-1)
        o_win = jnp.einsum('bghqk,bkgd->bqghd', p_w.astype(v_w.dtype), v_w)
        gt = gc.reshape(b, qn, g, hg, 3)
        o = gt[..., 0:1] * o_cmp + gt[..., 1:2] * o_slc + gt[..., 2:3] * o_win
        return o.reshape(b, qn, h, dv)

    out = lax.map(chunk, (to_blocks(q, nq, qn), to_blocks(gates, nq, qn), jnp.arange(nq)))
    return from_blocks(out)


def token_mixer(h, w_in, mla_q_norm_g, mla_w_uq, mla_kv_norm_g, mla_w_ukv,
                diff_lambda, diff_norm_g, lam_init,
                cmp_pos, cmp_w1, cmp_b1, cmp_w2, gate_b, w_out,
                cs_mla, cs_diff, cs_nsa):
    b, t, _ = h.shape
    splits = np.cumsum(IN_SPLITS)[:-1].tolist()
    cq, ckv, kr, dq, dkk, dvv, nq, nkv, ng = jnp.split(h @ w_in, splits, axis=-1)

    qm = (rms_norm(cq, mla_q_norm_g) @ mla_w_uq).reshape(b, t, MLA_HEADS, MLA_NOPE_DIM + MLA_ROPE_DIM)
    q_nope = qm[..., :MLA_NOPE_DIM]
    q_rope = apply_rope(qm[..., MLA_NOPE_DIM:], *cs_mla)
    kvm = (rms_norm(ckv, mla_kv_norm_g) @ mla_w_ukv).reshape(b, t, MLA_HEADS, MLA_NOPE_DIM + MLA_V_DIM)
    k_nope = kvm[..., :MLA_NOPE_DIM]
    v_mla = kvm[..., MLA_NOPE_DIM:]
    k_rope = apply_rope(kr[:, :, None, :], *cs_mla)[:, :, 0, :]
    o_mla = mla_attention(q_nope, q_rope, k_nope, k_rope, v_mla)

    qd = apply_rope(dq.reshape(b, t, DIFF_HEADS * 2, DIFF_QK_DIM), *cs_diff).reshape(b, t, DIFF_HEADS, 2, DIFF_QK_DIM)
    kd = apply_rope(dkk.reshape(b, t, DIFF_HEADS * 2, DIFF_QK_DIM), *cs_diff).reshape(b, t, DIFF_HEADS, 2, DIFF_QK_DIM)
    vd = dvv.reshape(b, t, DIFF_HEADS, DIFF_V_DIM)
    lf = diff_lambda.astype(jnp.float32)
    lam = jnp.exp(jnp.sum(lf[0] * lf[1])) - jnp.exp(jnp.sum(lf[2] * lf[3])) + lam_init
    o_diff = rms_norm(diff_attention(qd, kd, vd, lam), diff_norm_g) * (1.0 - lam_init)

    qn_ = apply_rope(nq.reshape(b, t, NSA_HEADS, NSA_DK), *cs_nsa)
    kv = nkv.reshape(b, t, 3, 2, NSA_GROUPS, NSA_DK)
    k_all = apply_rope(kv[:, :, :, 0].reshape(b, t, 3 * NSA_GROUPS, NSA_DK), *cs_nsa).reshape(b, t, 3, NSA_GROUPS, NSA_DK)
    v_all = kv[:, :, :, 1]
    k_cmp = compress_tokens(k_all[:, :, 0], cmp_pos[0], cmp_w1[0], cmp_b1[0], cmp_w2[0])
    v_cmp = compress_tokens(v_all[:, :, 0], cmp_pos[1], cmp_w1[1], cmp_b1[1], cmp_w2[1])
    gates = jax.nn.sigmoid(ng + gate_b).reshape(b, t, NSA_HEADS, 3)
    o_nsa = nsa_attention(qn_, k_cmp, v_cmp, k_all[:, :, 1], v_all[:, :, 1],
                          k_all[:, :, 2], v_all[:, :, 2], gates)

    o = jnp.concatenate([o_mla.reshape(b, t, -1), o_diff.reshape(b, t, -1),
                         o_nsa.reshape(b, t, -1)], axis=-1)
    return o @ w_out


def setup_inputs(seed: int = 0) -> dict:
    key = jax.random.key(seed)
    ks = jax.random.split(key, 24)

    def nrm(k, shape, scale):
        return jax.random.normal(k, shape, jnp.float32) * scale

    def gain(k, shape):
        return 1.0 + 0.02 * jax.random.normal(k, shape, jnp.float32)

    return {
        'x': nrm(ks[0], (BATCH, SEQ, D_MODEL), 1.0),
        'ffn_norm_g': gain(ks[1], (DEPTH, 2, D_MODEL)),
        'ffn_w_gate': nrm(ks[2], (DEPTH, 2, D_MODEL, D_FF), D_MODEL ** -0.5),
        'ffn_w_up': nrm(ks[3], (DEPTH, 2, D_MODEL, D_FF), D_MODEL ** -0.5),
        'ffn_w_down': nrm(ks[4], (DEPTH, 2, D_FF, D_MODEL), D_FF ** -0.5),
        'mix_norm_g': gain(ks[5], (DEPTH, D_MODEL)),
        'w_in': nrm(ks[6], (DEPTH, D_MODEL, N_IN), D_MODEL ** -0.5),
        'mla_q_norm_g': gain(ks[7], (DEPTH, MLA_Q_LORA)),
        'mla_w_uq': nrm(ks[8], (DEPTH, MLA_Q_LORA, MLA_HEADS * (MLA_NOPE_DIM + MLA_ROPE_DIM)), MLA_Q_LORA ** -0.5),
        'mla_kv_norm_g': gain(ks[9], (DEPTH, MLA_KV_LORA)),
        'mla_w_ukv': nrm(ks[10], (DEPTH, MLA_KV_LORA, MLA_HEADS * (MLA_NOPE_DIM + MLA_V_DIM)), MLA_KV_LORA ** -0.5),
        'diff_lambda': nrm(ks[11], (DEPTH, 4, DIFF_QK_DIM), 0.1),
        'diff_norm_g': gain(ks[12], (DEPTH, DIFF_V_DIM)),
        'nsa_cmp_pos': nrm(ks[13], (DEPTH, 2, CMP_BLOCK, NSA_DK), 0.02),
        'nsa_cmp_w1': nrm(ks[14], (DEPTH, 2, CMP_BLOCK * NSA_DK, CMP_HIDDEN), (CMP_BLOCK * NSA_DK) ** -0.5),
        'nsa_cmp_b1': nrm(ks[15], (DEPTH, 2, CMP_HIDDEN), 0.01),
        'nsa_cmp_w2': nrm(ks[16], (DEPTH, 2, CMP_HIDDEN, NSA_DK), CMP_HIDDEN ** -0.5),
        'nsa_gate_b': nrm(ks[17], (DEPTH, NSA_HEADS * 3), 0.01),
        'w_out': nrm(ks[18], (DEPTH, D_MIX, D_MODEL), D_MIX ** -0.5),
        'final_norm_g': gain(ks[19], (D_MODEL,)),
    }


def reference(x, ffn_norm_g, ffn_w_gate, ffn_w_up, ffn_w_down, mix_norm_g, w_in,
              mla_q_norm_g, mla_w_uq, mla_kv_norm_g, mla_w_ukv, diff_lambda, diff_norm_g,
              nsa_cmp_pos, nsa_cmp_w1, nsa_cmp_b1, nsa_cmp_w2, nsa_gate_b, w_out,
              final_norm_g):
    t = x.shape[1]
    cs_mla = rope_tables(t, MLA_ROPE_DIM)
    cs_diff = rope_tables(t, DIFF_QK_DIM)
    cs_nsa = rope_tables(t, NSA_DK)
    for l in range(DEPTH):
        lam_init = 0.8 - 0.6 * math.exp(-0.3 * l)
        x = x + 0.5 * swiglu(rms_norm(x, ffn_norm_g[l, 0]), ffn_w_gate[l, 0],
                             ffn_w_up[l, 0], ffn_w_down[l, 0])
        x = x + token_mixer(rms_norm(x, mix_norm_g[l]), w_in[l],
                            mla_q_norm_g[l], mla_w_uq[l], mla_kv_norm_g[l], mla_w_ukv[l],
                            diff_lambda[l], diff_norm_g[l], lam_init,
                            nsa_cmp_pos[l], nsa_cmp_w1[l], nsa_cmp_b1[l], nsa_cmp_w2[l],
                            nsa_gate_b[l], w_out[l], cs_mla, cs_diff, cs_nsa)
        x = x + 0.5 * swiglu(rms_norm(x, ffn_norm_g[l, 1]), ffn_w_gate[l, 1],
                             ffn_w_up[l, 1], ffn_w_down[l, 1])
    return rms_norm(x, final_norm_g)
```

```python
import functools
import math

import numpy as np
import jax
import jax.numpy as jnp
from jax import lax
from jax.experimental import pallas as pl
from jax.experimental.pallas import tpu as pltpu

F32 = jnp.float32
MXU_DTYPE = jnp.bfloat16

LANES = 128
VMEM_LIMIT = 56 * 1024 * 1024

MLA_HEADS, MLA_NOPE, MLA_ROPE, MLA_V = 4, 64, 32, 64
DIFF_HEADS, DIFF_QK, DIFF_V = 4, 32, 64
NSA_HEADS, NSA_GROUPS, NSA_D = 8, 2, 64
CMP_BLOCK, CMP_STRIDE, CMP_HIDDEN = 32, 16, 256
SEL_BLOCK, SEL_TOPK, WINDOW = 64, 16, 512
ROPE_THETA, EPS, NEG, FORCE = 10000.0, 1e-6, -1e30, 1e4
HG = NSA_HEADS // NSA_GROUPS

P_CQ, P_CKV, P_DV, P_VCT, P_VS, P_VW, P_NG, P_END = 0, 256, 384, 640, 768, 1024, 1280, 1536
R_KR, R_DQ, R_DK, R_NQ, R_KCT, R_KS, R_KW, R_END = 0, 128, 384, 640, 1152, 1280, 1536, 1792
T_C32, T_S32, T_C64, T_S64, T_CQ, T_SQ, T_OH, T_END = 0, 128, 256, 384, 512, 640, 768, 896


def _cparams(sem):
    return pltpu.CompilerParams(dimension_semantics=sem, vmem_limit_bytes=VMEM_LIMIT)


def _const_spec(shape):
    n = len(shape)
    return pl.BlockSpec(shape, lambda *_: (0,) * n)


def _rms(x, g):
    return x * lax.rsqrt(jnp.mean(x * x, axis=-1, keepdims=True) + EPS) * g


def _dot(a, b):
    return jnp.dot(a, b, preferred_element_type=F32)


def _dot_nt(a, b):
    return lax.dot_general(a, b, (((1,), (1,)), ((), ())), preferred_element_type=F32)


def _ffn_body(x_ref, g_ref, wg_ref, wu_ref, wd_ref, *rest, f_chunk, final):
    if final:
        gf_ref, o_ref = rest
    else:
        (o_ref,) = rest
    x = x_ref[...]
    h = _rms(x, g_ref[...]).astype(MXU_DTYPE)
    d_ff = wg_ref.shape[1]
    acc = jnp.zeros(x.shape, F32)
    for c in range(d_ff // f_chunk):
        sl = slice(c * f_chunk, (c + 1) * f_chunk)
        gate = _dot(h, wg_ref[:, sl])
        up = _dot(h, wu_ref[:, sl])
        act = (gate * jax.nn.sigmoid(gate) * up).astype(MXU_DTYPE)
        acc = acc + _dot(act, wd_ref[sl, :])
    y = x + 0.5 * acc
    if final:
        y = _rms(y, gf_ref[...])
    o_ref[...] = y


def _ffn(x2, g, wg, wu, wd, final_g=None, tm=512, f_chunk=256):
    n, d = x2.shape
    d_ff = wg.shape[1]
    assert n % tm == 0 and d_ff % f_chunk == 0
    final = final_g is not None
    row = pl.BlockSpec((tm, d), lambda i: (i, 0))
    in_specs = [row, _const_spec((1, d)), _const_spec((d, d_ff)), _const_spec((d, d_ff)), _const_spec((d_ff, d))]
    args = [x2, g.reshape(1, d), wg.astype(MXU_DTYPE), wu.astype(MXU_DTYPE), wd.astype(MXU_DTYPE)]
    if final:
        in_specs.append(_const_spec((1, d)))
        args.append(final_g.reshape(1, d))
    return pl.pallas_call(
        functools.partial(_ffn_body, f_chunk=f_chunk, final=final),
        grid=(n // tm,), in_specs=in_specs, out_specs=row,
        out_shape=jax.ShapeDtypeStruct((n, d), F32),
        compiler_params=_cparams(("parallel",)), name="ffn_final" if final else "ffn",
    )(*args)


def _rot_cols(w, dim):
    k = w.shape[0]
    g = w.reshape(k, -1, dim)
    return jnp.concatenate([-g[..., dim // 2:], g[..., :dim // 2]], axis=-1).reshape(w.shape)


def _pad_groups(w, width):
    k = w.shape[0]
    g = w.reshape(k, -1, width)
    return jnp.pad(g, ((0, 0), (0, 0), (0, LANES - width))).reshape(k, -1)


def _mix_weights(w_in, w_uq, w_ukv):
    k = w_in.shape[0]
    o = np.cumsum([0, 256, 128, 32, 256, 256, 256, 512, 768, 24]).tolist()
    cq, ckv, kr, dq, dk, dv, nq, nkv, ng = [w_in[:, o[i]:o[i + 1]] for i in range(9)]
    nkv = nkv.reshape(k, 3, 2, NSA_GROUPS * NSA_D)
    kct, vct = nkv[:, 0, 0], nkv[:, 0, 1]
    ks, vs = nkv[:, 1, 0], nkv[:, 1, 1]
    kw, vw = nkv[:, 2, 0], nkv[:, 2, 1]
    ngp = _pad_groups(ng, HG * 3)
    plain = jnp.concatenate([cq, ckv, dv, vct, _pad_groups(vs, NSA_D), _pad_groups(vw, NSA_D), ngp], axis=1)
    kr_pad = lambda a: jnp.pad(a, ((0, 0), (MLA_NOPE, LANES - MLA_NOPE - MLA_ROPE)))

    def rope_cols(rot):
        r32 = (lambda a: _rot_cols(a, 32)) if rot else (lambda a: a)
        r64 = (lambda a: _rot_cols(a, 64)) if rot else (lambda a: a)
        return jnp.concatenate([kr_pad(r32(kr)), r32(dq), r32(dk), r64(nq), r64(kct),
                                _pad_groups(r64(ks), NSA_D), _pad_groups(r64(kw), NSA_D)], axis=1)

    w_cat = jnp.concatenate([plain, rope_cols(False), rope_cols(True)], axis=1).astype(MXU_DTYPE)
    assert w_cat.shape[1] == P_END + 2 * R_END
    kq = w_uq.shape[0]
    uq = w_uq.reshape(kq, MLA_HEADS, MLA_NOPE + MLA_ROPE)
    nope, rope = uq[..., :MLA_NOPE], uq[..., MLA_NOPE:]
    zpad = jnp.zeros((kq, MLA_HEADS, LANES - MLA_NOPE - MLA_ROPE), w_uq.dtype)
    qa = jnp.concatenate([nope, rope, zpad], axis=-1).reshape(kq, -1)
    rrot = jnp.concatenate([-rope[..., MLA_ROPE // 2:], rope[..., :MLA_ROPE // 2]], axis=-1)
    qb = jnp.concatenate([jnp.zeros_like(nope), rrot, zpad], axis=-1).reshape(kq, -1)
    wuq = jnp.concatenate([qa, qb], axis=1).astype(MXU_DTYPE)
    kk = w_ukv.shape[0]
    ukv = w_ukv.reshape(kk, MLA_HEADS, MLA_NOPE + MLA_V)
    kn = jnp.pad(ukv[..., :MLA_NOPE], ((0, 0), (0, 0), (0, LANES - MLA_NOPE))).reshape(kk, -1)
    vv = ukv[..., MLA_NOPE:].reshape(kk, -1)
    wukv = jnp.concatenate([kn, vv], axis=1).astype(MXU_DTYPE)
    return w_cat, wuq, wukv


def _rope_tables(t):
    def cs(dim):
        inv = ROPE_THETA ** (-jnp.arange(0, dim, 2, dtype=F32) / dim)
        ang = jnp.arange(t, dtype=F32)[:, None] * inv[None, :]
        c, s = jnp.cos(ang), jnp.sin(ang)
        return jnp.concatenate([c, c], axis=1), jnp.concatenate([s, s], axis=1)

    c32, s32 = cs(32)
    c64, s64 = cs(64)
    one, zero = jnp.ones((t, MLA_NOPE), F32), jnp.zeros((t, MLA_NOPE), F32)
    z32 = jnp.zeros((t, LANES - MLA_NOPE - MLA_ROPE), F32)
    blk = jnp.arange(t)[:, None] // SEL_BLOCK
    onehot = (blk == jnp.arange(LANES)[None, :] - NSA_D).astype(F32)
    return jnp.concatenate([jnp.tile(c32, (1, 4)), jnp.tile(s32, (1, 4)), jnp.tile(c64, (1, 2)),
                            jnp.tile(s64, (1, 2)), jnp.concatenate([one, c32, z32], axis=1),
                            jnp.concatenate([zero, s32, z32], axis=1), onehot], axis=1)


def _mixproj_body(x_ref, g_ref, w_ref, tab_ref, qg_ref, wuq_ref, kvg_ref, wukv_ref, gb_ref,
                  qm_ref, km_ref, vm_ref, qd_ref, kd_ref, vd_ref, qn_ref, kct_ref, vct_ref,
                  ks_ref, vs_ref, kw_ref, vw_ref, gt_ref):
    h = _rms(x_ref[...], g_ref[...]).astype(MXU_DTYPE)
    yp = _dot(h, w_ref[:, 0:P_END])
    yr = _dot(h, w_ref[:, P_END:P_END + R_END])
    yt = _dot(h, w_ref[:, P_END + R_END:P_END + 2 * R_END])
    tab = lambda o: tab_ref[:, o:o + LANES]

    def rope(off, c, s):
        return yr[:, off:off + LANES] * tab(c) + yt[:, off:off + LANES] * tab(s)

    def put(ref, j, val):
        ref[:, j * LANES:(j + 1) * LANES] = val.astype(ref.dtype)

    cqn = _rms(yp[:, P_CQ:P_CKV], qg_ref[...]).astype(MXU_DTYPE)
    qab = _dot(cqn, wuq_ref[...])
    q_scale = (MLA_NOPE + MLA_ROPE) ** -0.5
    nq_lanes = MLA_HEADS * LANES
    for hh in range(MLA_HEADS):
        sl = slice(hh * LANES, (hh + 1) * LANES)
        put(qm_ref, hh, (qab[:, sl] * tab(T_CQ) + qab[:, nq_lanes:][:, sl] * tab(T_SQ)) * q_scale)
    ckvn = _rms(yp[:, P_CKV:P_DV], kvg_ref[...]).astype(MXU_DTYPE)
    kv = _dot(ckvn, wukv_ref[...])
    k_rope = rope(R_KR, T_C32, T_S32)
    for hh in range(MLA_HEADS):
        put(km_ref, hh, kv[:, hh * LANES:(hh + 1) * LANES] + k_rope)
    vm_ref[...] = kv[:, nq_lanes:].astype(vm_ref.dtype)
    d_scale = DIFF_QK ** -0.5
    for j in range(2):
        put(qd_ref, j, rope(R_DQ + j * LANES, T_C32, T_S32) * d_scale)
        put(kd_ref, j, rope(R_DK + j * LANES, T_C32, T_S32))
    vd_ref[...] = yp[:, P_DV:P_VCT].astype(vd_ref.dtype)
    n_scale = NSA_D ** -0.5
    for j in range(4):
        put(qn_ref, j, rope(R_NQ + j * LANES, T_C64, T_S64) * n_scale)
    put(kct_ref, 0, rope(R_KCT, T_C64, T_S64))
    vct_ref[...] = yp[:, P_VCT:P_VS].astype(vct_ref.dtype)
    for j in range(NSA_GROUPS):
        put(ks_ref, j, rope(R_KS + j * LANES, T_C64, T_S64) + tab(T_OH))
        put(kw_ref, j, rope(R_KW + j * LANES, T_C64, T_S64))
    vs_ref[...] = yp[:, P_VS:P_VW].astype(vs_ref.dtype)
    vw_ref[...] = yp[:, P_VW:P_NG].astype(vw_ref.dtype)
    gt_ref[...] = jax.nn.sigmoid(yp[:, P_NG:P_END] + gb_ref[...])


def _mixproj(x2, t, g, w_cat, tab, qg, wuq, kvg, wukv, gate_b, tm=256):
    n, d = x2.shape
    assert t % tm == 0
    tpb = t // tm
    widths = [512, 512, 256, 256, 256, 256, 512, 128, 128, 256, 256, 256, 256]
    out_shape = [jax.ShapeDtypeStruct((n, w), MXU_DTYPE) for w in widths]
    out_shape.append(jax.ShapeDtypeStruct((n, 2 * LANES), F32))
    out_specs = [pl.BlockSpec((tm, s.shape[1]), lambda i: (i, 0)) for s in out_shape]
    gb = _pad_groups(gate_b.reshape(1, -1), HG * 3)
    in_specs = [pl.BlockSpec((tm, d), lambda i: (i, 0)), _const_spec((1, d)), _const_spec(w_cat.shape),
                pl.BlockSpec((tm, T_END), lambda i: (i % tpb, 0)),
                _const_spec((1, qg.shape[0])), _const_spec(wuq.shape),
                _const_spec((1, kvg.shape[0])), _const_spec(wukv.shape), _const_spec(gb.shape)]
    return pl.pallas_call(
        _mixproj_body, grid=(n // tm,), in_specs=in_specs, out_specs=out_specs, out_shape=out_shape,
        compiler_params=_cparams(("parallel",)), name="mixproj",
    )(x2, g.reshape(1, d), w_cat, tab, qg.reshape(1, -1), wuq, kvg.reshape(1, -1), wukv, gb)


def _online_update(s, v, m_ref, l_ref, acc_ref):
    m_old = m_ref[...]
    m_new = jnp.maximum(m_old, jnp.max(s, axis=-1, keepdims=True))
    alpha = jnp.exp(m_old - m_new)
    p = jnp.exp(s - m_new)
    l_ref[...] = alpha * l_ref[...] + jnp.sum(p, axis=-1, keepdims=True)
    acc_ref[...] = alpha * acc_ref[...] + _dot(p.astype(MXU_DTYPE), v)
    m_ref[...] = m_new


def _init_stats(m_ref, l_ref, acc_ref):
    m_ref[...] = jnp.full(m_ref.shape, NEG, F32)
    l_ref[...] = jnp.zeros(l_ref.shape, F32)
    acc_ref[...] = jnp.zeros(acc_ref.shape, F32)


def _mla_body(q_ref, k_ref, v_ref, o_ref, m_ref, l_ref, acc_ref, *, tq, tk):
    qi = pl.program_id(2)
    q_first = qi * tq
    n_full = q_first // tk
    _init_stats(m_ref, l_ref, acc_ref)
    qs = [q_ref[0, :, hh * LANES:(hh + 1) * LANES] for hh in range(2)]

    def tile(j, masked):
        k0 = pl.multiple_of(j * tk, tk)
        v = v_ref[0, pl.ds(k0, tk), :]
        for hh in range(2):
            s = _dot_nt(qs[hh], k_ref[0, pl.ds(k0, tk), hh * LANES:(hh + 1) * LANES])
            if masked:
                row = q_first + lax.broadcasted_iota(jnp.int32, s.shape, 0)
                col = k0 + lax.broadcasted_iota(jnp.int32, s.shape, 1)
                s = jnp.where(col <= row, s, NEG)
            _online_update(s, v, m_ref.at[hh], l_ref.at[hh], acc_ref.at[hh])

    def full_tile(j, carry):
        tile(j, False)
        return carry

    lax.fori_loop(0, n_full, full_tile, 0)
    tile(n_full, True)
    lane = lax.broadcasted_iota(jnp.int32, (tq, LANES), 1)
    o0 = acc_ref[0] / l_ref[0]
    o1 = acc_ref[1] / l_ref[1]
    o_ref[0] = jnp.where(lane < MLA_V, o0, o1).astype(o_ref.dtype)


def _mla_attention(q, k, v, tq=256, tk=512):
    b, t, _ = q.shape
    assert t % tk == 0 and tk % tq == 0
    pairs = MLA_HEADS // 2
    return pl.pallas_call(
        functools.partial(_mla_body, tq=tq, tk=tk),
        grid=(b, pairs, t // tq),
        in_specs=[pl.BlockSpec((1, tq, 2 * LANES), lambda bi, p, i: (bi, i, p)),
                  pl.BlockSpec((1, t, 2 * LANES), lambda bi, p, i: (bi, 0, p)),
                  pl.BlockSpec((1, t, LANES), lambda bi, p, i: (bi, 0, p))],
        out_specs=pl.BlockSpec((1, tq, LANES), lambda bi, p, i: (bi, i, p)),
        out_shape=jax.ShapeDtypeStruct((b, t, pairs * LANES), MXU_DTYPE),
        scratch_shapes=[pltpu.VMEM((2, tq, 1), F32), pltpu.VMEM((2, tq, 1), F32), pltpu.VMEM((2, tq, LANES), F32)],
        compiler_params=_cparams(("parallel", "parallel", "arbitrary")), name="mla_attn",
    )(q, k, v)


def _diff_body(q_ref, k_ref, v_ref, lam_ref, g_ref, o_ref, m_ref, l_ref, acc_ref, *, tq, tk, lam_init):
    qi = pl.program_id(2)
    q_first = qi * tq
    n_full = q_first // tk
    _init_stats(m_ref, l_ref, acc_ref)
    q = q_ref[0]
    grp = lax.broadcasted_iota(jnp.int32, q.shape, 1) // DIFF_QK
    q4 = jnp.concatenate([jnp.where(grp == c, q, jnp.zeros_like(q)) for c in range(4)], axis=0)

    def tile(j, masked):
        k0 = pl.multiple_of(j * tk, tk)
        s = _dot_nt(q4, k_ref[0, pl.ds(k0, tk), :])
        if masked:
            row = q_first + lax.broadcasted_iota(jnp.int32, s.shape, 0) % tq
            col = k0 + lax.broadcasted_iota(jnp.int32, s.shape, 1)
            s = jnp.where(col <= row, s, NEG)
        _online_update(s, v_ref[0, pl.ds(k0, tk), :], m_ref, l_ref, acc_ref)

    def full_tile(j, carry):
        tile(j, False)
        return carry

    lax.fori_loop(0, n_full, full_tile, 0)
    tile(n_full, True)
    lf = lam_ref[...]
    lam = (jnp.exp(jnp.sum(lf[0:1] * lf[1:2], keepdims=True))
           - jnp.exp(jnp.sum(lf[2:3] * lf[3:4], keepdims=True)) + lam_init)
    a = acc_ref[...] / l_ref[...]
    o_h0 = a[0:tq] - lam * a[tq:2 * tq]
    o_h1 = a[2 * tq:3 * tq] - lam * a[3 * tq:4 * tq]
    lane = lax.broadcasted_iota(jnp.int32, (tq, LANES), 1)
    first = lane < DIFF_V
    o = jnp.where(first, o_h0, o_h1)
    sq = o * o
    ss0 = jnp.sum(jnp.where(first, sq, 0.0), axis=-1, keepdims=True)
    ss1 = jnp.sum(jnp.where(first, 0.0, sq), axis=-1, keepdims=True)
    ms = jnp.where(first, ss0, ss1) * (1.0 / DIFF_V)
    o_ref[0] = (o * lax.rsqrt(ms + EPS) * g_ref[...] * (1.0 - lam_init)).astype(o_ref.dtype)


def _diff_attention(q, k, v, diff_lambda, norm_g, lam_init, tq=128, tk=512):
    b, t, _ = q.shape
    assert t % tk == 0 and tk % tq == 0
    pairs = DIFF_HEADS // 2
    lam_pad = jnp.pad(diff_lambda.astype(F32), ((0, 4), (0, LANES - DIFF_QK)))
    g2 = jnp.tile(norm_g.astype(F32), 2).reshape(1, LANES)
    spec_qo = pl.BlockSpec((1, tq, LANES), lambda bi, p, i: (bi, i, p))
    spec_kv = pl.BlockSpec((1, t, LANES), lambda bi, p, i: (bi, 0, p))
    return pl.pallas_call(
        functools.partial(_diff_body, tq=tq, tk=tk, lam_init=lam_init),
        grid=(b, pairs, t // tq),
        in_specs=[spec_qo, spec_kv, spec_kv, _const_spec(lam_pad.shape), _const_spec(g2.shape)],
        out_specs=spec_qo,
        out_shape=jax.ShapeDtypeStruct((b, t, pairs * LANES), MXU_DTYPE),
        scratch_shapes=[pltpu.VMEM((4 * tq, 1), F32), pltpu.VMEM((4 * tq, 1), F32), pltpu.VMEM((4 * tq, LANES), F32)],
        compiler_params=_cparams(("parallel", "parallel", "arbitrary")), name="diff_attn",
    )(q, k, v, lam_pad, g2)


def _compress_body(xk_ref, xv_ref, w1_ref, w1o_ref, pos_ref, b1_ref, w2_ref, kc_ref, vc_ref):
    for kv, (x_ref, o_ref) in enumerate(((xk_ref, kc_ref), (xv_ref, vc_ref))):
        x = x_ref[0]
        posb = _dot(jnp.broadcast_to(pos_ref[kv], (8, pos_ref.shape[-1])), w1o_ref[kv])[0:1] + b1_ref[kv]
        for g in range(NSA_GROUPS):
            ab = _dot(x, w1_ref[kv, g])
            first, second = ab[:, :CMP_HIDDEN], ab[:, CMP_HIDDEN:]
            nxt = jnp.concatenate([second[1:], second[:1]], axis=0)
            hid = first + nxt + posb
            act = (hid * jax.nn.sigmoid(hid)).astype(MXU_DTYPE)
            o_ref[0, :, g * LANES:(g + 1) * LANES] = _dot(act, w2_ref[kv]).astype(o_ref.dtype)


def _compress(kct, vct, pos, w1, b1, w2):
    b, t, _ = kct.shape
    nch = t // CMP_STRIDE
    half = CMP_BLOCK // 2
    assert half == CMP_STRIDE
    xk = kct.reshape(b, nch, CMP_STRIDE * NSA_GROUPS * NSA_D)
    xv = vct.reshape(b, nch, CMP_STRIDE * NSA_GROUPS * NSA_D)
    w1r = w1.reshape(2, 2, half, NSA_D, CMP_HIDDEN)
    both = jnp.concatenate([w1r[:, 0], w1r[:, 1]], axis=-1)
    eye = jnp.eye(NSA_GROUPS, dtype=w1.dtype)
    w1g = jnp.einsum("klde,gh->kgldhe", both, eye)
    w1g = w1g.transpose(0, 1, 2, 4, 3, 5).reshape(2, NSA_GROUPS, half * NSA_GROUPS * NSA_D, 2 * CMP_HIDDEN)
    w2p = jnp.pad(w2, ((0, 0), (0, 0), (0, LANES - NSA_D)))
    posf = pos.reshape(2, 1, CMP_BLOCK * NSA_D)
    args = [xk, xv, w1g.astype(MXU_DTYPE), w1.astype(MXU_DTYPE), posf.astype(MXU_DTYPE),
            b1.reshape(2, 1, CMP_HIDDEN).astype(F32), w2p.astype(MXU_DTYPE)]
    x_spec = pl.BlockSpec((1, nch, xk.shape[-1]), lambda bi: (bi, 0, 0))
    o_spec = pl.BlockSpec((1, nch, NSA_GROUPS * LANES), lambda bi: (bi, 0, 0))
    return pl.pallas_call(
        _compress_body, grid=(b,),
        in_specs=[x_spec, x_spec] + [_const_spec(a.shape) for a in args[2:]],
        out_specs=[o_spec, o_spec],
        out_shape=[jax.ShapeDtypeStruct((b, nch, NSA_GROUPS * LANES), MXU_DTYPE)] * 2,
        compiler_params=_cparams(("parallel",)), name="nsa_compress",
    )(*args)


def _sel_map_t(nch):
    r_sel, r_cmp = SEL_BLOCK // CMP_STRIDE, CMP_BLOCK // CMP_STRIDE
    nc = nch - 1
    mt = np.zeros((nch * CMP_STRIDE // SEL_BLOCK, nch), np.float32)
    for j in range(mt.shape[0]):
        for m in range(r_sel):
            for n in range(r_cmp):
                idx = j * r_sel - m - n
                if 0 <= idx < nc:
                    mt[j, idx] += 1.0
    return mt


def _split3(x):
    hi = x.astype(MXU_DTYPE)
    r1 = x - hi.astype(F32)
    mid = r1.astype(MXU_DTYPE)
    lo = (r1 - mid.astype(F32)).astype(MXU_DTYPE)
    return hi, mid, lo


def _nsa_body(q_ref, kc_ref, vc_ref, ks_ref, vs_ref, kw_ref, vw_ref, gt_ref, mt_ref, o_ref,
              sc_ref, m_ref, l_ref, acc_ref, *, tk, win_rows):
    qn = SEL_BLOCK
    rows = HG * qn
    ci = pl.program_id(2)
    q_first = ci * qn
    lane = lax.broadcasted_iota(jnp.int32, (rows, LANES), 1)
    tpos = q_first + lax.broadcasted_iota(jnp.int32, (rows, 1), 0) % qn
    qe = q_ref[0, :, 0:LANES].astype(F32)
    qo = q_ref[0, :, LANES:2 * LANES].astype(F32)
    q4 = jnp.concatenate([qe, pltpu.roll(qe, NSA_D, 1), qo, pltpu.roll(qo, NSA_D, 1)], axis=0)
    is_q = lane < NSA_D

    s = _dot_nt(jnp.where(is_q, q4, 0.0).astype(MXU_DTYPE), kc_ref[0])
    n_idx = lax.broadcasted_iota(jnp.int32, s.shape, 1)
    valid = n_idx * CMP_STRIDE + (CMP_BLOCK - 1) <= tpos
    sm = jnp.where(valid, s, NEG)
    e = jnp.where(valid, jnp.exp(sm - jnp.max(sm, axis=-1, keepdims=True)), 0.0)
    den = jnp.sum(e, axis=-1, keepdims=True)
    p = e * jnp.where(den > 0.0, 1.0 / den, 0.0)
    o_cmp = _dot(p.astype(MXU_DTYPE), vc_ref[0])

    pg = p[0:qn] + p[qn:2 * qn] + p[2 * qn:3 * qn] + p[3 * qn:4 * qn]
    mt = mt_ref[...]
    p_slc_t = sum(_dot_nt(mt, part) for part in _split3(pg))
    nb = p_slc_t.shape[0]
    jrow = lax.broadcasted_iota(jnp.int32, (nb, qn), 0)
    forced = (jrow == 0) | (jrow == ci) | (jrow == ci - 1)
    score = jnp.where(jrow > ci, -1.0, p_slc_t + jnp.where(forced, FORCE, 0.0))
    sc_ref[...] = score

    def rank_step(i, cnt):
        row = sc_ref[pl.ds(i, 1), :]
        ahead = (row > score) | ((row == score) & (i < jrow))
        return cnt + jnp.where(ahead, 1.0, 0.0)

    cnt = lax.fori_loop(0, ci + 1, rank_step, jnp.zeros((nb, qn), F32))
    sel_t = (cnt < float(SEL_TOPK)) & (jrow <= ci)
    neg_t = jnp.where(sel_t, 0.0, NEG)
    pieces = [jnp.zeros((NSA_D, LANES), F32), jnp.concatenate([neg_t, jnp.zeros((nb, LANES - qn), F32)], axis=1)]
    if LANES - NSA_D - nb:
        pieces.append(jnp.zeros((LANES - NSA_D - nb, LANES), F32))
    sq = jnp.concatenate(pieces, axis=0)
    bias = sq.T[0:qn]
    q_aug = jnp.where(is_q, q4, jnp.concatenate([bias] * HG, axis=0)).astype(MXU_DTYPE)

    _init_stats(m_ref, l_ref, acc_ref)
    n_full = q_first // tk

    def slc_tile(j, masked):
        k0 = pl.multiple_of(j * tk, tk)
        s = _dot_nt(q_aug, ks_ref[0, pl.ds(k0, tk), :])
        if masked:
            col = k0 + lax.broadcasted_iota(jnp.int32, s.shape, 1)
            s = jnp.where(col <= tpos, s, NEG)
        _online_update(s, vs_ref[0, pl.ds(k0, tk), :], m_ref, l_ref, acc_ref)

    def slc_full(j, carry):
        slc_tile(j, False)
        return carry

    lax.fori_loop(0, n_full, slc_full, 0)
    slc_tile(n_full, True)
    o_slc = acc_ref[...] / l_ref[...]

    w0 = pl.multiple_of(jnp.maximum(q_first - WINDOW, 0), qn)
    s = _dot_nt(q_aug, kw_ref[0, pl.ds(w0, win_rows), :])
    delta = tpos - (w0 + lax.broadcasted_iota(jnp.int32, s.shape, 1))
    sm = jnp.where((delta >= 0) & (delta < WINDOW), s, NEG)
    e = jnp.exp(sm - jnp.max(sm, axis=-1, keepdims=True))
    o_win = _dot(e.astype(MXU_DTYPE), vw_ref[0, pl.ds(w0, win_rows), :]) / jnp.sum(e, axis=-1, keepdims=True)

    gt = gt_ref[0]
    glane = lax.broadcasted_iota(jnp.int32, gt.shape, 1)
    gate = lambda c: jnp.sum(jnp.where(glane == c, gt, 0.0), axis=-1, keepdims=True)
    outs = []
    for hl in range(HG):
        r = slice(hl * qn, (hl + 1) * qn)
        outs.append(gate(3 * hl) * o_cmp[r] + gate(3 * hl + 1) * o_slc[r] + gate(3 * hl + 2) * o_win[r])
    first = glane < NSA_D
    for pr in range(HG // 2):
        pair = jnp.where(first, outs[2 * pr], pltpu.roll(outs[2 * pr + 1], NSA_D, 1))
        o_ref[0, :, pr * LANES:(pr + 1) * LANES] = pair.astype(o_ref.dtype)


def _nsa_attention(q, kc, vc, ks, vs, kw, vw, gates, tk=512):
    b, t, _ = q.shape
    qn = SEL_BLOCK
    nch = kc.shape[1]
    win_rows = WINDOW + qn
    assert t % tk == 0 and tk % qn == 0 and t >= win_rows and HG == 4
    mt = jnp.asarray(_sel_map_t(nch), MXU_DTYPE)
    nb = mt.shape[0]
    rows = HG * qn
    spec_q = pl.BlockSpec((1, qn, 2 * LANES), lambda bi, g, i: (bi, i, g))
    spec_c = pl.BlockSpec((1, nch, LANES), lambda bi, g, i: (bi, 0, g))
    spec_t = pl.BlockSpec((1, t, LANES), lambda bi, g, i: (bi, 0, g))
    spec_g = pl.BlockSpec((1, qn, LANES), lambda bi, g, i: (bi, i, g))
    return pl.pallas_call(
        functools.partial(_nsa_body, tk=tk, win_rows=win_rows),
        grid=(b, NSA_GROUPS, t // qn),
        in_specs=[spec_q, spec_c, spec_c, spec_t, spec_t, spec_t, spec_t, spec_g, _const_spec(mt.shape)],
        out_specs=spec_q,
        out_shape=jax.ShapeDtypeStruct((b, t, NSA_GROUPS * 2 * LANES), MXU_DTYPE),
        scratch_shapes=[pltpu.VMEM((nb, qn), F32), pltpu.VMEM((rows, 1), F32), pltpu.VMEM((rows, 1), F32),
                        pltpu.VMEM((rows, LANES), F32)],
        compiler_params=_cparams(("parallel", "parallel", "arbitrary")), name="nsa_attn",
    )(q, kc, vc, ks, vs, kw, vw, gates, mt)


def _outproj_body(x_ref, om_ref, od_ref, on_ref, w_ref, o_ref):
    a, b = om_ref.shape[1], om_ref.shape[1] + od_ref.shape[1]
    o_ref[...] = (x_ref[...] + _dot(om_ref[...], w_ref[0:a, :]) + _dot(od_ref[...], w_ref[a:b, :])
                  + _dot(on_ref[...], w_ref[b:, :]))


def _outproj(x2, om, od, on, w_out, tm=512):
    n, d = x2.shape
    row = lambda w: pl.BlockSpec((tm, w), lambda i: (i, 0))
    return pl.pallas_call(
        _outproj_body, grid=(n // tm,),
        in_specs=[row(d), row(om.shape[1]), row(od.shape[1]), row(on.shape[1]), _const_spec(w_out.shape)],
        out_specs=row(d), out_shape=jax.ShapeDtypeStruct((n, d), F32),
        compiler_params=_cparams(("parallel",)), name="outproj",
    )(x2, om, od, on, w_out.astype(MXU_DTYPE))


def kernel(x, ffn_norm_g, ffn_w_gate, ffn_w_up, ffn_w_down, mix_norm_g, w_in, mla_q_norm_g, mla_w_uq, mla_kv_norm_g, mla_w_ukv, diff_lambda, diff_norm_g, nsa_cmp_pos, nsa_cmp_w1, nsa_cmp_b1, nsa_cmp_w2, nsa_gate_b, w_out, final_norm_g):
    b, t, d = x.shape
    depth = w_in.shape[0]
    tab = _rope_tables(t)
    x2 = x.reshape(b * t, d)
    for l in range(depth):
        lam_init = 0.8 - 0.6 * math.exp(-0.3 * l)
        x2 = _ffn(x2, ffn_norm_g[l, 0], ffn_w_gate[l, 0], ffn_w_up[l, 0], ffn_w_down[l, 0])
        w_cat, wuq, wukv = _mix_weights(w_in[l], mla_w_uq[l], mla_w_ukv[l])
        (qm, km, vm, qd, kd, vd, qn, kct, vct, ks, vs, kw, vw, gt) = _mixproj(
            x2, t, mix_norm_g[l], w_cat, tab, mla_q_norm_g[l], wuq, mla_kv_norm_g[l], wukv, nsa_gate_b[l])
        sh = lambda a: a.reshape(b, t, a.shape[-1])
        o_mla = _mla_attention(sh(qm), sh(km), sh(vm))
        o_diff = _diff_attention(sh(qd), sh(kd), sh(vd), diff_lambda[l], diff_norm_g[l], lam_init)
        kc, vc = _compress(sh(kct), sh(vct), nsa_cmp_pos[l], nsa_cmp_w1[l], nsa_cmp_b1[l], nsa_cmp_w2[l])
        o_nsa = _nsa_attention(sh(qn), kc, vc, sh(ks), sh(vs), sh(kw), sh(vw), sh(gt))
        flat = lambda a: a.reshape(b * t, a.shape[-1])
        x2 = _outproj(x2, flat(o_mla), flat(o_diff), flat(o_nsa), w_out[l])
        last = l == depth - 1
        x2 = _ffn(x2, ffn_norm_g[l, 1], ffn_w_gate[l, 1], ffn_w_up[l, 1], ffn_w_down[l, 1],
                  final_g=final_norm_g if last else None)
    return x2.reshape(b, t, d)
```

```python
import functools
import math

import numpy as np
import jax
import jax.numpy as jnp
from jax import lax
from jax.experimental import pallas as pl
from jax.experimental.pallas import tpu as pltpu

F32 = jnp.float32
MXU_DTYPE = jnp.bfloat16

LANES = 128
VMEM_LIMIT = 56 * 1024 * 1024

MLA_HEADS, MLA_NOPE, MLA_ROPE, MLA_V = 4, 64, 32, 64
DIFF_HEADS, DIFF_QK, DIFF_V = 4, 32, 64
NSA_HEADS, NSA_GROUPS, NSA_D = 8, 2, 64
CMP_BLOCK, CMP_STRIDE, CMP_HIDDEN = 32, 16, 256
SEL_BLOCK, SEL_TOPK, WINDOW = 64, 16, 512
ROPE_THETA, EPS, NEG, FORCE = 10000.0, 1e-6, -1e30, 1e4
HG = NSA_HEADS // NSA_GROUPS
HEAD_V = 64
assert MLA_V == DIFF_V == NSA_D == HEAD_V

P_CQ, P_CKV, P_DV, P_VCT, P_VS, P_VW, P_NG, P_END = 0, 256, 384, 896, 1024, 1280, 1536, 1792
R_KR, R_DQ, R_DK, R_NQ, R_KCT, R_KS, R_KW, R_END = 0, 128, 384, 640, 1152, 1280, 1536, 1792
T_C32, T_S32, T_C64, T_S64, T_CQ, T_SQ, T_OH, T_END = 0, 128, 256, 384, 512, 640, 768, 896


def _cparams(sem):
    return pltpu.CompilerParams(dimension_semantics=sem, vmem_limit_bytes=VMEM_LIMIT)


def _const_spec(shape):
    n = len(shape)
    return pl.BlockSpec(shape, lambda *_: (0,) * n)


def _rms(x, g):
    return x * lax.rsqrt(jnp.mean(x * x, axis=-1, keepdims=True) + EPS) * g


def _dot(a, b):
    return jnp.dot(a, b, preferred_element_type=F32)


def _dot_nt(a, b):
    return lax.dot_general(a, b, (((1,), (1,)), ((), ())), preferred_element_type=F32)


def _ffn_body(x_ref, g_ref, wg_ref, wu_ref, wd_ref, *rest, f_chunk, final):
    if final:
        gf_ref, o_ref = rest
    else:
        (o_ref,) = rest
    x = x_ref[...]
    h = _rms(x, g_ref[...]).astype(MXU_DTYPE)
    d_ff = wg_ref.shape[1]
    acc = jnp.zeros(x.shape, F32)
    for c in range(d_ff // f_chunk):
        sl = slice(c * f_chunk, (c + 1) * f_chunk)
        gate = _dot(h, wg_ref[:, sl])
        up = _dot(h, wu_ref[:, sl])
        act = (gate * jax.nn.sigmoid(gate) * up).astype(MXU_DTYPE)
        acc = acc + _dot(act, wd_ref[sl, :])
    y = x + 0.5 * acc
    if final:
        y = _rms(y, gf_ref[...])
    o_ref[...] = y


def _ffn(x2, g, wg, wu, wd, final_g=None, tm=512, f_chunk=256):
    n, d = x2.shape
    d_ff = wg.shape[1]
    assert n % tm == 0 and d_ff % f_chunk == 0
    final = final_g is not None
    row = pl.BlockSpec((tm, d), lambda i: (i, 0))
    in_specs = [row, _const_spec((1, d)), _const_spec((d, d_ff)), _const_spec((d, d_ff)), _const_spec((d_ff, d))]
    args = [x2, g.reshape(1, d), wg.astype(MXU_DTYPE), wu.astype(MXU_DTYPE), wd.astype(MXU_DTYPE)]
    if final:
        in_specs.append(_const_spec((1, d)))
        args.append(final_g.reshape(1, d))
    return pl.pallas_call(
        functools.partial(_ffn_body, f_chunk=f_chunk, final=final),
        grid=(n // tm,), in_specs=in_specs, out_specs=row,
        out_shape=jax.ShapeDtypeStruct((n, d), F32),
        compiler_params=_cparams(("parallel",)), name="ffn_final" if final else "ffn",
    )(*args)


def _rot_cols(w, dim):
    k = w.shape[0]
    g = w.reshape(k, -1, dim)
    return jnp.concatenate([-g[..., dim // 2:], g[..., :dim // 2]], axis=-1).reshape(w.shape)


def _pad_groups(w, width):
    k = w.shape[0]
    g = w.reshape(k, -1, width)
    return jnp.pad(g, ((0, 0), (0, 0), (0, LANES - width))).reshape(k, -1)


def _mix_weights(w_in, w_uq, w_ukv):
    k = w_in.shape[0]
    o = np.cumsum([0, 256, 128, 32, 256, 256, 256, 512, 768, 24]).tolist()
    cq, ckv, kr, dq, dk, dv, nq, nkv, ng = [w_in[:, o[i]:o[i + 1]] for i in range(9)]
    nkv = nkv.reshape(k, 3, 2, NSA_GROUPS * NSA_D)
    kct, vct = nkv[:, 0, 0], nkv[:, 0, 1]
    ks, vs = nkv[:, 1, 0], nkv[:, 1, 1]
    kw, vw = nkv[:, 2, 0], nkv[:, 2, 1]
    ngp = _pad_groups(ng, HG * 3)
    plain = jnp.concatenate([cq, ckv, _pad_groups(dv, HEAD_V), vct, _pad_groups(vs, HEAD_V),
                             _pad_groups(vw, HEAD_V), ngp], axis=1)
    kr_pad = lambda a: jnp.pad(a, ((0, 0), (MLA_NOPE, LANES - MLA_NOPE - MLA_ROPE)))

    def rope_cols(rot):
        r32 = (lambda a: _rot_cols(a, 32)) if rot else (lambda a: a)
        r64 = (lambda a: _rot_cols(a, 64)) if rot else (lambda a: a)
        return jnp.concatenate([kr_pad(r32(kr)), r32(dq), r32(dk), r64(nq), r64(kct),
                                _pad_groups(r64(ks), NSA_D), _pad_groups(r64(kw), NSA_D)], axis=1)

    w_cat = jnp.concatenate([plain, rope_cols(False), rope_cols(True)], axis=1).astype(MXU_DTYPE)
    assert w_cat.shape[1] == P_END + 2 * R_END
    kq = w_uq.shape[0]
    uq = w_uq.reshape(kq, MLA_HEADS, MLA_NOPE + MLA_ROPE)
    nope, rope = uq[..., :MLA_NOPE], uq[..., MLA_NOPE:]
    zpad = jnp.zeros((kq, MLA_HEADS, LANES - MLA_NOPE - MLA_ROPE), w_uq.dtype)
    qa = jnp.concatenate([nope, rope, zpad], axis=-1).reshape(kq, -1)
    rrot = jnp.concatenate([-rope[..., MLA_ROPE // 2:], rope[..., :MLA_ROPE // 2]], axis=-1)
    qb = jnp.concatenate([jnp.zeros_like(nope), rrot, zpad], axis=-1).reshape(kq, -1)
    wuq = jnp.concatenate([qa, qb], axis=1).astype(MXU_DTYPE)
    kk = w_ukv.shape[0]
    ukv = w_ukv.reshape(kk, MLA_HEADS, MLA_NOPE + MLA_V)
    pad_head = lambda a: jnp.pad(a, ((0, 0), (0, 0), (0, LANES - a.shape[-1]))).reshape(kk, -1)
    wukv = jnp.concatenate([pad_head(ukv[..., :MLA_NOPE]), pad_head(ukv[..., MLA_NOPE:])], axis=1)
    return w_cat, wuq, wukv.astype(MXU_DTYPE)


def _rope_tables(t):
    def cs(dim):
        inv = ROPE_THETA ** (-jnp.arange(0, dim, 2, dtype=F32) / dim)
        ang = jnp.arange(t, dtype=F32)[:, None] * inv[None, :]
        c, s = jnp.cos(ang), jnp.sin(ang)
        return jnp.concatenate([c, c], axis=1), jnp.concatenate([s, s], axis=1)

    c32, s32 = cs(32)
    c64, s64 = cs(64)
    one, zero = jnp.ones((t, MLA_NOPE), F32), jnp.zeros((t, MLA_NOPE), F32)
    z32 = jnp.zeros((t, LANES - MLA_NOPE - MLA_ROPE), F32)
    blk = jnp.arange(t)[:, None] // SEL_BLOCK
    onehot = (blk == jnp.arange(LANES)[None, :] - NSA_D).astype(F32)
    return jnp.concatenate([jnp.tile(c32, (1, 4)), jnp.tile(s32, (1, 4)), jnp.tile(c64, (1, 2)),
                            jnp.tile(s64, (1, 2)), jnp.concatenate([one, c32, z32], axis=1),
                            jnp.concatenate([zero, s32, z32], axis=1), onehot], axis=1)


def _mixproj_body(x_ref, g_ref, w_ref, tab_ref, qg_ref, wuq_ref, kvg_ref, wukv_ref, gb_ref,
                  qm_ref, km_ref, vm_ref, qd_ref, kd_ref, vd_ref, qn_ref, kct_ref, vct_ref,
                  ks_ref, vs_ref, kw_ref, vw_ref, gt_ref):
    h = _rms(x_ref[...], g_ref[...]).astype(MXU_DTYPE)
    yp = _dot(h, w_ref[:, 0:P_END])
    yr = _dot(h, w_ref[:, P_END:P_END + R_END])
    yt = _dot(h, w_ref[:, P_END + R_END:P_END + 2 * R_END])
    tab = lambda o: tab_ref[:, o:o + LANES]
    ones_hi = (lax.broadcasted_iota(jnp.int32, (1, LANES), 1) >= HEAD_V).astype(F32)

    def rope(off, c, s):
        return yr[:, off:off + LANES] * tab(c) + yt[:, off:off + LANES] * tab(s)

    def put(ref, j, val):
        ref[:, j * LANES:(j + 1) * LANES] = val.astype(ref.dtype)

    cqn = _rms(yp[:, P_CQ:P_CKV], qg_ref[...]).astype(MXU_DTYPE)
    qab = _dot(cqn, wuq_ref[...])
    q_scale = (MLA_NOPE + MLA_ROPE) ** -0.5
    nq_lanes = MLA_HEADS * LANES
    for hh in range(MLA_HEADS):
        sl = slice(hh * LANES, (hh + 1) * LANES)
        put(qm_ref, hh, (qab[:, sl] * tab(T_CQ) + qab[:, nq_lanes:][:, sl] * tab(T_SQ)) * q_scale)
    ckvn = _rms(yp[:, P_CKV:P_DV], kvg_ref[...]).astype(MXU_DTYPE)
    kv = _dot(ckvn, wukv_ref[...])
    k_rope = rope(R_KR, T_C32, T_S32)
    for hh in range(MLA_HEADS):
        put(km_ref, hh, kv[:, hh * LANES:(hh + 1) * LANES] + k_rope)
        put(vm_ref, hh, kv[:, nq_lanes:][:, hh * LANES:(hh + 1) * LANES] + ones_hi)
    d_scale = DIFF_QK ** -0.5
    for j in range(2):
        put(qd_ref, j, rope(R_DQ + j * LANES, T_C32, T_S32) * d_scale)
        put(kd_ref, j, rope(R_DK + j * LANES, T_C32, T_S32))
    for hh in range(DIFF_HEADS):
        put(vd_ref, hh, yp[:, P_DV + hh * LANES:P_DV + (hh + 1) * LANES] + ones_hi)
    n_scale = NSA_D ** -0.5
    for j in range(4):
        put(qn_ref, j, rope(R_NQ + j * LANES, T_C64, T_S64) * n_scale)
    put(kct_ref, 0, rope(R_KCT, T_C64, T_S64))
    vct_ref[...] = yp[:, P_VCT:P_VS].astype(vct_ref.dtype)
    for j in range(NSA_GROUPS):
        put(ks_ref, j, rope(R_KS + j * LANES, T_C64, T_S64) + tab(T_OH))
        put(kw_ref, j, rope(R_KW + j * LANES, T_C64, T_S64))
        put(vs_ref, j, yp[:, P_VS + j * LANES:P_VS + (j + 1) * LANES] + ones_hi)
        put(vw_ref, j, yp[:, P_VW + j * LANES:P_VW + (j + 1) * LANES] + ones_hi)
    gt_ref[...] = jax.nn.sigmoid(yp[:, P_NG:P_END] + gb_ref[...])


def _mixproj(x2, t, g, w_cat, tab, qg, wuq, kvg, wukv, gate_b, tm=256):
    n, d = x2.shape
    assert t % tm == 0
    tpb = t // tm
    widths = [512, 512, 512, 256, 256, 512, 512, 128, 128, 256, 256, 256, 256]
    out_shape = [jax.ShapeDtypeStruct((n, w), MXU_DTYPE) for w in widths]
    out_shape.append(jax.ShapeDtypeStruct((n, 2 * LANES), F32))
    out_specs = [pl.BlockSpec((tm, s.shape[1]), lambda i: (i, 0)) for s in out_shape]
    gb = _pad_groups(gate_b.reshape(1, -1), HG * 3)
    in_specs = [pl.BlockSpec((tm, d), lambda i: (i, 0)), _const_spec((1, d)), _const_spec(w_cat.shape),
                pl.BlockSpec((tm, T_END), lambda i: (i % tpb, 0)),
                _const_spec((1, qg.shape[0])), _const_spec(wuq.shape),
                _const_spec((1, kvg.shape[0])), _const_spec(wukv.shape), _const_spec(gb.shape)]
    return pl.pallas_call(
        _mixproj_body, grid=(n // tm,), in_specs=in_specs, out_specs=out_specs, out_shape=out_shape,
        compiler_params=_cparams(("parallel",)), name="mixproj",
    )(x2, g.reshape(1, d), w_cat, tab, qg.reshape(1, -1), wuq, kvg.reshape(1, -1), wukv, gb)


def _init_stats(m_ref, acc_ref):
    m_ref[...] = jnp.full(m_ref.shape, NEG, F32)
    acc_ref[...] = jnp.zeros(acc_ref.shape, F32)


def _tile_probs(s, m_ref):
    m_old = m_ref[...]
    m_new = jnp.maximum(m_old, jnp.max(s, axis=-1, keepdims=True))
    m_ref[...] = m_new
    p = jnp.exp(s - pltpu.repeat(m_new, s.shape[1] // LANES, axis=1))
    return p.astype(MXU_DTYPE), jnp.exp(m_old - m_new)


def _normalised(acc):
    return acc / pltpu.roll(acc, HEAD_V, 1)


def _causal_bias(q_first, k_first, nq, nk):
    row = q_first + lax.broadcasted_iota(jnp.int32, (nq, nk), 0)
    col = k_first + lax.broadcasted_iota(jnp.int32, (nq, nk), 1)
    return jnp.where(col <= row, 0.0, NEG)


def _pair_lanes(o_even, o_odd):
    lane = lax.broadcasted_iota(jnp.int32, o_even.shape, 1)
    return jnp.where(lane < HEAD_V, o_even, pltpu.roll(o_odd, HEAD_V, 1))


def _mla_body(q_ref, k_ref, v_ref, o_ref, m_ref, acc_ref, *, tq, tk):
    q_first = pl.program_id(2) * tq
    n_full = q_first // tk
    _init_stats(m_ref, acc_ref)
    qs = [q_ref[0, :, hh * LANES:(hh + 1) * LANES] for hh in range(2)]

    def tile(j, bias):
        k0 = pl.multiple_of(j * tk, tk)
        for hh in range(2):
            lanes = slice(hh * LANES, (hh + 1) * LANES)
            s = _dot_nt(qs[hh], k_ref[0, pl.ds(k0, tk), lanes])
            if bias is not None:
                s = s + bias
            p, alpha = _tile_probs(s, m_ref.at[hh])
            acc_ref[hh] = alpha * acc_ref[hh] + _dot(p, v_ref[0, pl.ds(k0, tk), lanes])

    def full_tile(j, carry):
        tile(j, None)
        return carry

    lax.fori_loop(0, n_full, full_tile, 0)
    tile(n_full, _causal_bias(q_first, n_full * tk, tq, tk))
    o_ref[0] = _pair_lanes(_normalised(acc_ref[0]), _normalised(acc_ref[1])).astype(o_ref.dtype)


def _mla_attention(q, k, v, tq=512, tk=512):
    b, t, _ = q.shape
    assert t % tk == 0 and tk % tq == 0
    pairs = MLA_HEADS // 2
    spec_t = pl.BlockSpec((1, t, 2 * LANES), lambda bi, p, i: (bi, 0, p))
    return pl.pallas_call(
        functools.partial(_mla_body, tq=tq, tk=tk),
        grid=(b, pairs, t // tq),
        in_specs=[pl.BlockSpec((1, tq, 2 * LANES), lambda bi, p, i: (bi, i, p)), spec_t, spec_t],
        out_specs=pl.BlockSpec((1, tq, LANES), lambda bi, p, i: (bi, i, p)),
        out_shape=jax.ShapeDtypeStruct((b, t, pairs * LANES), MXU_DTYPE),
        scratch_shapes=[pltpu.VMEM((2, tq, LANES), F32), pltpu.VMEM((2, tq, LANES), F32)],
        compiler_params=_cparams(("parallel", "parallel", "arbitrary")), name="mla_attn",
    )(q, k, v)


def _diff_body(q_ref, k_ref, v_ref, lam_ref, g_ref, o_ref, m_ref, acc_ref, *, tq, tk, lam_init):
    q_first = pl.program_id(2) * tq
    n_full = q_first // tk
    _init_stats(m_ref, acc_ref)
    q = q_ref[0]
    grp = lax.broadcasted_iota(jnp.int32, q.shape, 1) // DIFF_QK
    q4 = jnp.concatenate([jnp.where(grp == c, q, jnp.zeros_like(q)) for c in range(4)], axis=0)
    head_rows = [slice(0, 2 * tq), slice(2 * tq, 4 * tq)]

    def tile(j, bias):
        k0 = pl.multiple_of(j * tk, tk)
        s = _dot_nt(q4, k_ref[0, pl.ds(k0, tk), :])
        if bias is not None:
            s = s + jnp.concatenate([bias] * 4, axis=0)
        p, alpha = _tile_probs(s, m_ref)
        for hl, r in enumerate(head_rows):
            acc_ref[r, :] = alpha[r] * acc_ref[r, :] + _dot(p[r], v_ref[0, pl.ds(k0, tk), hl * LANES:(hl + 1) * LANES])

    def full_tile(j, carry):
        tile(j, None)
        return carry

    lax.fori_loop(0, n_full, full_tile, 0)
    tile(n_full, _causal_bias(q_first, n_full * tk, tq, tk))
    lf = lam_ref[...]
    lam = (jnp.exp(jnp.sum(lf[0:1] * lf[1:2], keepdims=True))
           - jnp.exp(jnp.sum(lf[2:3] * lf[3:4], keepdims=True)) + lam_init)
    a = _normalised(acc_ref[...])
    lane = lax.broadcasted_iota(jnp.int32, (tq, LANES), 1)
    normed = []
    for hl in range(2):
        o = a[2 * hl * tq:(2 * hl + 1) * tq] - lam * a[(2 * hl + 1) * tq:(2 * hl + 2) * tq]
        ms = jnp.sum(jnp.where(lane < DIFF_V, o * o, 0.0), axis=-1, keepdims=True) * (1.0 / DIFF_V)
        normed.append(o * lax.rsqrt(ms + EPS))
    o_ref[0] = (_pair_lanes(normed[0], normed[1]) * g_ref[...] * (1.0 - lam_init)).astype(o_ref.dtype)


def _diff_attention(q, k, v, diff_lambda, norm_g, lam_init, tq=256, tk=512):
    b, t, _ = q.shape
    assert t % tk == 0 and tk % tq == 0
    pairs = DIFF_HEADS // 2
    lam_pad = jnp.pad(diff_lambda.astype(F32), ((0, 4), (0, LANES - DIFF_QK)))
    g2 = jnp.tile(norm_g.astype(F32), 2).reshape(1, LANES)
    spec_qo = pl.BlockSpec((1, tq, LANES), lambda bi, p, i: (bi, i, p))
    return pl.pallas_call(
        functools.partial(_diff_body, tq=tq, tk=tk, lam_init=lam_init),
        grid=(b, pairs, t // tq),
        in_specs=[spec_qo, pl.BlockSpec((1, t, LANES), lambda bi, p, i: (bi, 0, p)),
                  pl.BlockSpec((1, t, 2 * LANES), lambda bi, p, i: (bi, 0, p)),
                  _const_spec(lam_pad.shape), _const_spec(g2.shape)],
        out_specs=spec_qo,
        out_shape=jax.ShapeDtypeStruct((b, t, pairs * LANES), MXU_DTYPE),
        scratch_shapes=[pltpu.VMEM((4 * tq, LANES), F32), pltpu.VMEM((4 * tq, LANES), F32)],
        compiler_params=_cparams(("parallel", "parallel", "arbitrary")), name="diff_attn",
    )(q, k, v, lam_pad, g2)


def _compress_body(xk_ref, xv_ref, w1_ref, w1o_ref, pos_ref, b1_ref, w2_ref, kc_ref, vc_ref):
    for kv, (x_ref, o_ref) in enumerate(((xk_ref, kc_ref), (xv_ref, vc_ref))):
        x = x_ref[0]
        posb = _dot(jnp.broadcast_to(pos_ref[kv], (8, pos_ref.shape[-1])), w1o_ref[kv])[0:1] + b1_ref[kv]
        for g in range(NSA_GROUPS):
            ab = _dot(x, w1_ref[kv, g])
            first, second = ab[:, :CMP_HIDDEN], ab[:, CMP_HIDDEN:]
            nxt = jnp.concatenate([second[1:], second[:1]], axis=0)
            hid = first + nxt + posb
            act = (hid * jax.nn.sigmoid(hid)).astype(MXU_DTYPE)
            o_ref[0, :, g * LANES:(g + 1) * LANES] = _dot(act, w2_ref[kv]).astype(o_ref.dtype)


def _compress(kct, vct, pos, w1, b1, w2):
    b, t, _ = kct.shape
    nch = t // CMP_STRIDE
    half = CMP_BLOCK // 2
    assert half == CMP_STRIDE
    xk = kct.reshape(b, nch, CMP_STRIDE * NSA_GROUPS * NSA_D)
    xv = vct.reshape(b, nch, CMP_STRIDE * NSA_GROUPS * NSA_D)
    w1r = w1.reshape(2, 2, half, NSA_D, CMP_HIDDEN)
    both = jnp.concatenate([w1r[:, 0], w1r[:, 1]], axis=-1)
    eye = jnp.eye(NSA_GROUPS, dtype=w1.dtype)
    w1g = jnp.einsum("klde,gh->kgldhe", both, eye)
    w1g = w1g.transpose(0, 1, 2, 4, 3, 5).reshape(2, NSA_GROUPS, half * NSA_GROUPS * NSA_D, 2 * CMP_HIDDEN)
    w2p = jnp.pad(w2, ((0, 0), (0, 0), (0, LANES - NSA_D)))
    posf = pos.reshape(2, 1, CMP_BLOCK * NSA_D)
    args = [xk, xv, w1g.astype(MXU_DTYPE), w1.astype(MXU_DTYPE), posf.astype(MXU_DTYPE),
            b1.reshape(2, 1, CMP_HIDDEN).astype(F32), w2p.astype(MXU_DTYPE)]
    x_spec = pl.BlockSpec((1, nch, xk.shape[-1]), lambda bi: (bi, 0, 0))
    o_spec = pl.BlockSpec((1, nch, NSA_GROUPS * LANES), lambda bi: (bi, 0, 0))
    return pl.pallas_call(
        _compress_body, grid=(b,),
        in_specs=[x_spec, x_spec] + [_const_spec(a.shape) for a in args[2:]],
        out_specs=[o_spec, o_spec],
        out_shape=[jax.ShapeDtypeStruct((b, nch, NSA_GROUPS * LANES), MXU_DTYPE)] * 2,
        compiler_params=_cparams(("parallel",)), name="nsa_compress",
    )(*args)


def _sel_map_t(nch):
    r_sel, r_cmp = SEL_BLOCK // CMP_STRIDE, CMP_BLOCK // CMP_STRIDE
    nc = nch - 1
    mt = np.zeros((nch * CMP_STRIDE // SEL_BLOCK, nch), np.float32)
    for j in range(mt.shape[0]):
        for m in range(r_sel):
            for n in range(r_cmp):
                idx = j * r_sel - m - n
                if 0 <= idx < nc:
                    mt[j, idx] += 1.0
    return mt


def _split3(x):
    hi = x.astype(MXU_DTYPE)
    r1 = x - hi.astype(F32)
    mid = r1.astype(MXU_DTYPE)
    lo = (r1 - mid.astype(F32)).astype(MXU_DTYPE)
    return hi, mid, lo


def _nsa_body(q_ref, kc_ref, vc_ref, ks_ref, vs_ref, kw_ref, vw_ref, gt_ref, mt_ref, o_ref,
              sc_ref, m_ref, acc_ref, *, nq, tk, win_rows):
    rows = HG * nq
    q_first = pl.program_id(2) * nq
    blk_first = q_first // SEL_BLOCK
    lane = lax.broadcasted_iota(jnp.int32, (rows, LANES), 1)
    qpos = q_first + lax.broadcasted_iota(jnp.int32, (nq, 1), 0)
    tpos = jnp.concatenate([qpos] * HG, axis=0)
    qe = q_ref[0, :, 0:LANES].astype(F32)
    qo = q_ref[0, :, LANES:2 * LANES].astype(F32)
    q4 = jnp.concatenate([qe, pltpu.roll(qe, NSA_D, 1), qo, pltpu.roll(qo, NSA_D, 1)], axis=0)
    is_q = lane < NSA_D

    s = _dot_nt(jnp.where(is_q, q4, 0.0).astype(MXU_DTYPE), kc_ref[0])
    n_idx = lax.broadcasted_iota(jnp.int32, s.shape, 1)
    valid = n_idx * CMP_STRIDE + (CMP_BLOCK - 1) <= tpos
    sm = jnp.where(valid, s, NEG)
    e = jnp.where(valid, jnp.exp(sm - jnp.max(sm, axis=-1, keepdims=True)), 0.0)
    den = jnp.sum(e, axis=-1, keepdims=True)
    p = e * jnp.where(den > 0.0, 1.0 / den, 0.0)
    o_cmp = _dot(p.astype(MXU_DTYPE), vc_ref[0])

    pg = p[0:nq]
    for hl in range(1, HG):
        pg = pg + p[hl * nq:(hl + 1) * nq]
    mt = mt_ref[...]
    p_slc_t = sum(_dot_nt(mt, part) for part in _split3(pg))
    nb = p_slc_t.shape[0]
    jrow = lax.broadcasted_iota(jnp.int32, (nb, nq), 0)
    cblk = blk_first + lax.broadcasted_iota(jnp.int32, (nb, nq), 1) // SEL_BLOCK
    forced = (jrow == 0) | (jrow == cblk) | (jrow == cblk - 1)
    score = jnp.where(jrow > cblk, -1.0, p_slc_t + jnp.where(forced, FORCE, 0.0))
    sc_ref[...] = score

    def rank_step(i, cnt):
        row = sc_ref[pl.ds(i, 1), :]
        ahead = (row > score) | ((row == score) & (i < jrow))
        return cnt + jnp.where(ahead, 1.0, 0.0)

    cnt = lax.fori_loop(0, blk_first + nq // SEL_BLOCK, rank_step, jnp.zeros((nb, nq), F32))
    sel_t = (cnt < float(SEL_TOPK)) & (jrow <= cblk)
    neg_t = jnp.where(sel_t, 0.0, NEG)
    bias = []
    for c in range(nq // LANES):
        pieces = [jnp.zeros((NSA_D, LANES), F32), neg_t[:, c * LANES:(c + 1) * LANES]]
        if LANES - NSA_D - nb:
            pieces.append(jnp.zeros((LANES - NSA_D - nb, LANES), F32))
        bias.append(jnp.concatenate(pieces, axis=0).T)
    bias = jnp.concatenate(bias, axis=0)
    q_aug = jnp.where(is_q, q4, jnp.concatenate([bias] * HG, axis=0)).astype(MXU_DTYPE)

    _init_stats(m_ref, acc_ref)
    n_full = q_first // tk

    def slc_tile(j, tile_bias):
        k0 = pl.multiple_of(j * tk, tk)
        s = _dot_nt(q_aug, ks_ref[0, pl.ds(k0, tk), :])
        if tile_bias is not None:
            s = s + jnp.concatenate([tile_bias] * HG, axis=0)
        p, alpha = _tile_probs(s, m_ref)
        acc_ref[...] = alpha * acc_ref[...] + _dot(p, vs_ref[0, pl.ds(k0, tk), :])

    def slc_full(j, carry):
        slc_tile(j, None)
        return carry

    lax.fori_loop(0, n_full, slc_full, 0)
    slc_tile(n_full, _causal_bias(q_first, n_full * tk, nq, tk))
    o_slc = _normalised(acc_ref[...])

    w0 = pl.multiple_of(jnp.maximum(q_first - WINDOW, 0), nq)
    delta = qpos - (w0 + lax.broadcasted_iota(jnp.int32, (nq, win_rows), 1))
    wbias = jnp.where((delta >= 0) & (delta < WINDOW), 0.0, NEG)
    s = _dot_nt(q_aug, kw_ref[0, pl.ds(w0, win_rows), :]) + jnp.concatenate([wbias] * HG, axis=0)
    e = jnp.exp(s - jnp.max(s, axis=-1, keepdims=True))
    o_win = _normalised(_dot(e.astype(MXU_DTYPE), vw_ref[0, pl.ds(w0, win_rows), :]))

    gt = gt_ref[0]
    glane = lax.broadcasted_iota(jnp.int32, gt.shape, 1)
    gate = lambda c: jnp.sum(jnp.where(glane == c, gt, 0.0), axis=-1, keepdims=True)
    outs = []
    for hl in range(HG):
        r = slice(hl * nq, (hl + 1) * nq)
        outs.append(gate(3 * hl) * o_cmp[r] + gate(3 * hl + 1) * o_slc[r] + gate(3 * hl + 2) * o_win[r])
    for pr in range(HG // 2):
        o_ref[0, :, pr * LANES:(pr + 1) * LANES] = _pair_lanes(outs[2 * pr], outs[2 * pr + 1]).astype(o_ref.dtype)


def _nsa_attention(q, kc, vc, ks, vs, kw, vw, gates, nq=256, tk=512):
    b, t, _ = q.shape
    nch = kc.shape[1]
    win_rows = WINDOW + nq
    assert t % tk == 0 and tk % nq == 0 and nq % LANES == 0 and t >= win_rows and HG == 4
    mt = jnp.asarray(_sel_map_t(nch), MXU_DTYPE)
    nb = mt.shape[0]
    assert nb <= LANES - NSA_D
    rows = HG * nq
    spec_q = pl.BlockSpec((1, nq, 2 * LANES), lambda bi, g, i: (bi, i, g))
    spec_c = pl.BlockSpec((1, nch, LANES), lambda bi, g, i: (bi, 0, g))
    spec_t = pl.BlockSpec((1, t, LANES), lambda bi, g, i: (bi, 0, g))
    spec_g = pl.BlockSpec((1, nq, LANES), lambda bi, g, i: (bi, i, g))
    return pl.pallas_call(
        functools.partial(_nsa_body, nq=nq, tk=tk, win_rows=win_rows),
        grid=(b, NSA_GROUPS, t // nq),
        in_specs=[spec_q, spec_c, spec_c, spec_t, spec_t, spec_t, spec_t, spec_g, _const_spec(mt.shape)],
        out_specs=spec_q,
        out_shape=jax.ShapeDtypeStruct((b, t, NSA_GROUPS * 2 * LANES), MXU_DTYPE),
        scratch_shapes=[pltpu.VMEM((nb, nq), F32), pltpu.VMEM((rows, LANES), F32), pltpu.VMEM((rows, LANES), F32)],
        compiler_params=_cparams(("parallel", "parallel", "arbitrary")), name="nsa_attn",
    )(q, kc, vc, ks, vs, kw, vw, gates, mt)


def _outproj_body(x_ref, om_ref, od_ref, on_ref, w_ref, o_ref):
    a, b = om_ref.shape[1], om_ref.shape[1] + od_ref.shape[1]
    o_ref[...] = (x_ref[...] + _dot(om_ref[...], w_ref[0:a, :]) + _dot(od_ref[...], w_ref[a:b, :])
                  + _dot(on_ref[...], w_ref[b:, :]))


def _outproj(x2, om, od, on, w_out, tm=512):
    n, d = x2.shape
    row = lambda w: pl.BlockSpec((tm, w), lambda i: (i, 0))
    return pl.pallas_call(
        _outproj_body, grid=(n // tm,),
        in_specs=[row(d), row(om.shape[1]), row(od.shape[1]), row(on.shape[1]), _const_spec(w_out.shape)],
        out_specs=row(d), out_shape=jax.ShapeDtypeStruct((n, d), F32),
        compiler_params=_cparams(("parallel",)), name="outproj",
    )(x2, om, od, on, w_out.astype(MXU_DTYPE))


def kernel(x, ffn_norm_g, ffn_w_gate, ffn_w_up, ffn_w_down, mix_norm_g, w_in, mla_q_norm_g, mla_w_uq, mla_kv_norm_g, mla_w_ukv, diff_lambda, diff_norm_g, nsa_cmp_pos, nsa_cmp_w1, nsa_cmp_b1, nsa_cmp_w2, nsa_gate_b, w_out, final_norm_g):
    b, t, d = x.shape
    depth = w_in.shape[0]
    tab = _rope_tables(t)
    x2 = x.reshape(b * t, d)
    for l in range(depth):
        lam_init = 0.8 - 0.6 * math.exp(-0.3 * l)
        x2 = _ffn(x2, ffn_norm_g[l, 0], ffn_w_gate[l, 0], ffn_w_up[l, 0], ffn_w_down[l, 0])
        w_cat, wuq, wukv = _mix_weights(w_in[l], mla_w_uq[l], mla_w_ukv[l])
        (qm, km, vm, qd, kd, vd, qn, kct, vct, ks, vs, kw, vw, gt) = _mixproj(
            x2, t, mix_norm_g[l], w_cat, tab, mla_q_norm_g[l], wuq, mla_kv_norm_g[l], wukv, nsa_gate_b[l])
        sh = lambda a: a.reshape(b, t, a.shape[-1])
        o_mla = _mla_attention(sh(qm), sh(km), sh(vm))
        o_diff = _diff_attention(sh(qd), sh(kd), sh(vd), diff_lambda[l], diff_norm_g[l], lam_init)
        kc, vc = _compress(sh(kct), sh(vct), nsa_cmp_pos[l], nsa_cmp_w1[l], nsa_cmp_b1[l], nsa_cmp_w2[l])
        o_nsa = _nsa_attention(sh(qn), kc, vc, sh(ks), sh(vs), sh(kw), sh(vw), sh(gt))
        flat = lambda a: a.reshape(b * t, a.shape[-1])
        x2 = _outproj(x2, flat(o_mla), flat(o_diff), flat(o_nsa), w_out[l])
        last = l == depth - 1
        x2 = _ffn(x2, ffn_norm_g[l, 1], ffn_w_gate[l, 1], ffn_w_up[l, 1], ffn_w_down[l, 1],
                  final_g=final_norm_g if last else None)
    return x2.reshape(b, t, d)
```

```python
import functools
import math

import numpy as np
import jax
import jax.numpy as jnp
from jax import lax
from jax.experimental import pallas as pl
from jax.experimental.pallas import tpu as pltpu

F32 = jnp.float32
MXU_DTYPE = jnp.bfloat16

LANES = 128
VMEM_LIMIT = 56 * 1024 * 1024

MLA_HEADS, MLA_NOPE, MLA_ROPE, MLA_V = 4, 64, 32, 64
DIFF_HEADS, DIFF_QK, DIFF_V = 4, 32, 64
NSA_HEADS, NSA_GROUPS, NSA_D = 8, 2, 64
CMP_BLOCK, CMP_STRIDE, CMP_HIDDEN = 32, 16, 256
SEL_BLOCK, SEL_TOPK, WINDOW = 64, 16, 512
ROPE_THETA, EPS, NEG, FORCE = 10000.0, 1e-6, -1e30, 1e4
HG = NSA_HEADS // NSA_GROUPS
HEAD_V = 64
assert MLA_V == DIFF_V == NSA_D == HEAD_V

P_CQ, P_CKV, P_DV, P_VCT, P_VS, P_VW, P_NG, P_END = 0, 256, 384, 896, 1024, 1280, 1536, 1792
R_KR, R_DQ, R_DK, R_NQ, R_KCT, R_KS, R_KW, R_END = 0, 128, 384, 640, 1152, 1280, 1536, 1792
T_C32, T_S32, T_C64, T_S64, T_CQ, T_SQ, T_OH, T_END = 0, 128, 256, 384, 512, 640, 768, 896


def _cparams(sem):
    return pltpu.CompilerParams(dimension_semantics=sem, vmem_limit_bytes=VMEM_LIMIT)


def _const_spec(shape):
    n = len(shape)
    return pl.BlockSpec(shape, lambda *_: (0,) * n)


def _rms(x, g):
    return x * lax.rsqrt(jnp.mean(x * x, axis=-1, keepdims=True) + EPS) * g


def _dot(a, b):
    return jnp.dot(a, b, preferred_element_type=F32)


def _dot_nt(a, b):
    return lax.dot_general(a, b, (((1,), (1,)), ((), ())), preferred_element_type=F32)


def _ffn_body(*refs, f_chunk, mixed, final):
    refs = list(refs)
    x_ref = refs.pop(0)
    x = x_ref[...]
    if mixed:
        o_refs, wo_ref = refs[:3], refs[3]
        refs = refs[4:]
        row = 0
        for o_in in o_refs:
            x = x + _dot(o_in[...], wo_ref[row:row + o_in.shape[1], :])
            row += o_in.shape[1]
    g_ref, wg_ref, wu_ref, wd_ref = refs[:4]
    gf_ref = refs[4] if final else None
    o_ref = refs[-1]
    h = _rms(x, g_ref[...]).astype(MXU_DTYPE)
    d_ff = wg_ref.shape[1]
    acc = jnp.zeros(x.shape, F32)
    for c in range(d_ff // f_chunk):
        sl = slice(c * f_chunk, (c + 1) * f_chunk)
        gate = _dot(h, wg_ref[:, sl])
        up = _dot(h, wu_ref[:, sl])
        act = (gate * jax.nn.sigmoid(gate) * up).astype(MXU_DTYPE)
        acc = acc + _dot(act, wd_ref[sl, :])
    y = x + 0.5 * acc
    if final:
        y = _rms(y, gf_ref[...])
    o_ref[...] = y


def _ffn(x2, g, wg, wu, wd, mix=None, final_g=None, tm=512, f_chunk=256):
    n, d = x2.shape
    d_ff = wg.shape[1]
    assert n % tm == 0 and d_ff % f_chunk == 0
    final, mixed = final_g is not None, mix is not None
    row = lambda w: pl.BlockSpec((tm, w), lambda i: (i, 0))
    in_specs, args = [row(d)], [x2]
    if mixed:
        *outs, w_out = mix
        in_specs += [row(o.shape[1]) for o in outs] + [_const_spec(w_out.shape)]
        args += list(outs) + [w_out]
    in_specs += [_const_spec((1, d)), _const_spec((d, d_ff)), _const_spec((d, d_ff)), _const_spec((d_ff, d))]
    args += [g.reshape(1, d), wg, wu, wd]
    if final:
        in_specs.append(_const_spec((1, d)))
        args.append(final_g.reshape(1, d))
    name = "ffn" + ("_mix" if mixed else "") + ("_final" if final else "")
    return pl.pallas_call(
        functools.partial(_ffn_body, f_chunk=f_chunk, mixed=mixed, final=final),
        grid=(n // tm,), in_specs=in_specs, out_specs=row(d),
        out_shape=jax.ShapeDtypeStruct((n, d), F32),
        compiler_params=_cparams(("parallel",)), name=name,
    )(*args)


def _pad_groups(w, width, lead=0):
    g = w.reshape(w.shape[:-1] + (-1, width))
    g = jnp.pad(g, [(0, 0)] * (g.ndim - 1) + [(lead, LANES - width - lead)])
    return g.reshape(w.shape[:-1] + (-1,))


def _mix_weights(w_in, w_uq, w_ukv):
    o = np.cumsum([0, 256, 128, 32, 256, 256, 256, 512, 768, 24]).tolist()
    cq, ckv, kr, dq, dk, dv, nq, nkv, ng = [w_in[..., o[i]:o[i + 1]] for i in range(9)]
    nkv = nkv.reshape(nkv.shape[:-1] + (3, 2, NSA_GROUPS * NSA_D))
    kct, vct = nkv[..., 0, 0, :], nkv[..., 0, 1, :]
    ks, vs = nkv[..., 1, 0, :], nkv[..., 1, 1, :]
    kw, vw = nkv[..., 2, 0, :], nkv[..., 2, 1, :]
    w_cat = jnp.concatenate(
        [cq, ckv, _pad_groups(dv, HEAD_V), vct, _pad_groups(vs, HEAD_V), _pad_groups(vw, HEAD_V),
         _pad_groups(ng, HG * 3),
         _pad_groups(kr, MLA_ROPE, lead=MLA_NOPE), dq, dk, nq, kct, _pad_groups(ks, NSA_D), _pad_groups(kw, NSA_D)],
        axis=-1).astype(MXU_DTYPE)
    assert w_cat.shape[-1] == P_END + R_END
    wuq = _pad_groups(w_uq, MLA_NOPE + MLA_ROPE).astype(MXU_DTYPE)
    ukv = w_ukv.reshape(w_ukv.shape[:-1] + (MLA_HEADS, MLA_NOPE + MLA_V))
    flat = lambda a: a.reshape(a.shape[:-2] + (-1,))
    wukv = jnp.concatenate([_pad_groups(flat(ukv[..., :MLA_NOPE]), MLA_NOPE),
                            _pad_groups(flat(ukv[..., MLA_NOPE:]), MLA_V)], axis=-1).astype(MXU_DTYPE)
    return w_cat, wuq, wukv


def _rope_tables(t):
    def cs(dim):
        inv = ROPE_THETA ** (-jnp.arange(0, dim, 2, dtype=F32) / dim)
        ang = jnp.arange(t, dtype=F32)[:, None] * inv[None, :]
        c, s = jnp.cos(ang), jnp.sin(ang)
        return jnp.concatenate([c, c], axis=1), jnp.concatenate([-s, s], axis=1)

    c32, s32 = cs(32)
    c64, s64 = cs(64)
    one, zero = jnp.ones((t, MLA_NOPE), F32), jnp.zeros((t, MLA_NOPE), F32)
    z32 = jnp.zeros((t, LANES - MLA_NOPE - MLA_ROPE), F32)
    blk = jnp.arange(t)[:, None] // SEL_BLOCK
    onehot = (blk == jnp.arange(LANES)[None, :] - NSA_D).astype(F32)
    return jnp.concatenate([jnp.tile(c32, (1, 4)), jnp.tile(s32, (1, 4)), jnp.tile(c64, (1, 2)),
                            jnp.tile(s64, (1, 2)), jnp.concatenate([one, c32, z32], axis=1),
                            jnp.concatenate([zero, s32, z32], axis=1), onehot], axis=1)


def _mixproj_body(x_ref, g_ref, w_ref, tab_ref, qg_ref, wuq_ref, kvg_ref, wukv_ref, gb_ref,
                  qm_ref, km_ref, vm_ref, qd_ref, kd_ref, vd_ref, qn_ref, kct_ref, vct_ref,
                  ks_ref, vs_ref, kw_ref, vw_ref, gt_ref):
    h = _rms(x_ref[...], g_ref[...]).astype(MXU_DTYPE)
    yp = _dot(h, w_ref[:, 0:P_END])
    yr = _dot(h, w_ref[:, P_END:P_END + R_END])
    tab = lambda o: tab_ref[:, o:o + LANES]
    lane = lax.broadcasted_iota(jnp.int32, (1, LANES), 1)
    ones_hi = (lane >= HEAD_V).astype(F32)

    def swap_halves(x, dim):
        half = dim // 2
        return jnp.where(lane % dim < half, pltpu.roll(x, LANES - half, 1), pltpu.roll(x, half, 1))

    def rope_of(x, dim, c, s):
        return x * tab(c) + swap_halves(x, dim) * tab(s)

    def rope(off, dim, c, s):
        return rope_of(yr[:, off:off + LANES], dim, c, s)

    def put(ref, j, val):
        ref[:, j * LANES:(j + 1) * LANES] = val.astype(ref.dtype)

    cqn = _rms(yp[:, P_CQ:P_CKV], qg_ref[...]).astype(MXU_DTYPE)
    qa = _dot(cqn, wuq_ref[...])
    q_scale = (MLA_NOPE + MLA_ROPE) ** -0.5
    nq_lanes = MLA_HEADS * LANES
    for hh in range(MLA_HEADS):
        put(qm_ref, hh, rope_of(qa[:, hh * LANES:(hh + 1) * LANES], MLA_ROPE, T_CQ, T_SQ) * q_scale)
    ckvn = _rms(yp[:, P_CKV:P_DV], kvg_ref[...]).astype(MXU_DTYPE)
    kv = _dot(ckvn, wukv_ref[...])
    k_rope = rope(R_KR, MLA_ROPE, T_C32, T_S32)
    for hh in range(MLA_HEADS):
        put(km_ref, hh, kv[:, hh * LANES:(hh + 1) * LANES] + k_rope)
        put(vm_ref, hh, kv[:, nq_lanes:][:, hh * LANES:(hh + 1) * LANES] + ones_hi)
    d_scale = DIFF_QK ** -0.5
    for j in range(2):
        put(qd_ref, j, rope(R_DQ + j * LANES, DIFF_QK, T_C32, T_S32) * d_scale)
        put(kd_ref, j, rope(R_DK + j * LANES, DIFF_QK, T_C32, T_S32))
    for hh in range(DIFF_HEADS):
        put(vd_ref, hh, yp[:, P_DV + hh * LANES:P_DV + (hh + 1) * LANES] + ones_hi)
    n_scale = NSA_D ** -0.5
    for j in range(4):
        put(qn_ref, j, rope(R_NQ + j * LANES, NSA_D, T_C64, T_S64) * n_scale)
    put(kct_ref, 0, rope(R_KCT, NSA_D, T_C64, T_S64))
    vct_ref[...] = yp[:, P_VCT:P_VS].astype(vct_ref.dtype)
    for j in range(NSA_GROUPS):
        put(ks_ref, j, rope(R_KS + j * LANES, NSA_D, T_C64, T_S64) + tab(T_OH))
        put(kw_ref, j, rope(R_KW + j * LANES, NSA_D, T_C64, T_S64))
        put(vs_ref, j, yp[:, P_VS + j * LANES:P_VS + (j + 1) * LANES] + ones_hi)
        put(vw_ref, j, yp[:, P_VW + j * LANES:P_VW + (j + 1) * LANES] + ones_hi)
    gt_ref[...] = jax.nn.sigmoid(yp[:, P_NG:P_END] + gb_ref[...])


def _mixproj(x2, t, g, w_cat, tab, qg, wuq, kvg, wukv, gate_b, tm=256):
    n, d = x2.shape
    assert t % tm == 0
    tpb = t // tm
    widths = [512, 512, 512, 256, 256, 512, 512, 128, 128, 256, 256, 256, 256]
    out_shape = [jax.ShapeDtypeStruct((n, w), MXU_DTYPE) for w in widths]
    out_shape.append(jax.ShapeDtypeStruct((n, 2 * LANES), F32))
    out_specs = [pl.BlockSpec((tm, s.shape[1]), lambda i: (i, 0)) for s in out_shape]
    gb = _pad_groups(gate_b.reshape(1, -1), HG * 3)
    in_specs = [pl.BlockSpec((tm, d), lambda i: (i, 0)), _const_spec((1, d)), _const_spec(w_cat.shape),
                pl.BlockSpec((tm, T_END), lambda i: (i % tpb, 0)),
                _const_spec((1, qg.shape[0])), _const_spec(wuq.shape),
                _const_spec((1, kvg.shape[0])), _const_spec(wukv.shape), _const_spec(gb.shape)]
    return pl.pallas_call(
        _mixproj_body, grid=(n // tm,), in_specs=in_specs, out_specs=out_specs, out_shape=out_shape,
        compiler_params=_cparams(("parallel",)), name="mixproj",
    )(x2, g.reshape(1, d), w_cat, tab, qg.reshape(1, -1), wuq, kvg.reshape(1, -1), wukv, gb)


def _init_stats(m_ref, acc_ref):
    m_ref[...] = jnp.full(m_ref.shape, NEG, F32)
    acc_ref[...] = jnp.zeros(acc_ref.shape, F32)


def _tile_probs(s, m_ref):
    m_old = m_ref[...]
    m_new = jnp.maximum(m_old, jnp.max(s, axis=-1, keepdims=True))
    m_ref[...] = m_new
    p = jnp.exp(s - jnp.concatenate([m_new] * (s.shape[1] // LANES), axis=1))
    return p.astype(MXU_DTYPE), jnp.exp(m_old - m_new)


def _normalised(acc):
    return acc / pltpu.roll(acc, HEAD_V, 1)


def _causal_bias(q_first, k_first, nq, nk):
    row = q_first + lax.broadcasted_iota(jnp.int32, (nq, nk), 0)
    col = k_first + lax.broadcasted_iota(jnp.int32, (nq, nk), 1)
    return jnp.where(col <= row, 0.0, NEG)


def _pair_lanes(o_even, o_odd):
    lane = lax.broadcasted_iota(jnp.int32, o_even.shape, 1)
    return jnp.where(lane < HEAD_V, o_even, pltpu.roll(o_odd, HEAD_V, 1))


def _mla_body(q_ref, k_ref, v_ref, o_ref, m_ref, acc_ref, *, tq, tk, hps):
    q_first = pl.program_id(2) * tq
    n_full = q_first // tk
    _init_stats(m_ref, acc_ref)
    qs = [q_ref[0, :, hh * LANES:(hh + 1) * LANES] for hh in range(hps)]

    def tile(j, bias):
        k0 = pl.multiple_of(j * tk, tk)
        for hh in range(hps):
            lanes = slice(hh * LANES, (hh + 1) * LANES)
            s = _dot_nt(qs[hh], k_ref[0, pl.ds(k0, tk), lanes])
            if bias is not None:
                s = s + bias
            p, alpha = _tile_probs(s, m_ref.at[hh])
            acc_ref[hh] = alpha * acc_ref[hh] + _dot(p, v_ref[0, pl.ds(k0, tk), lanes])

    def full_tile(j, carry):
        tile(j, None)
        return carry

    lax.fori_loop(0, n_full, full_tile, 0)
    tile(n_full, _causal_bias(q_first, n_full * tk, tq, tk))
    for pr in range(hps // 2):
        pair = _pair_lanes(_normalised(acc_ref[2 * pr]), _normalised(acc_ref[2 * pr + 1]))
        o_ref[0, :, pr * LANES:(pr + 1) * LANES] = pair.astype(o_ref.dtype)


def _mla_attention(q, k, v, tq=512, tk=512, hps=4):
    b, t, _ = q.shape
    assert t % tk == 0 and tk % tq == 0 and MLA_HEADS % hps == 0 and hps % 2 == 0
    spec_t = pl.BlockSpec((1, t, hps * LANES), lambda bi, p, i: (bi, 0, p))
    return pl.pallas_call(
        functools.partial(_mla_body, tq=tq, tk=tk, hps=hps),
        grid=(b, MLA_HEADS // hps, t // tq),
        in_specs=[pl.BlockSpec((1, tq, hps * LANES), lambda bi, p, i: (bi, i, p)), spec_t, spec_t],
        out_specs=pl.BlockSpec((1, tq, hps // 2 * LANES), lambda bi, p, i: (bi, i, p)),
        out_shape=jax.ShapeDtypeStruct((b, t, MLA_HEADS // 2 * LANES), MXU_DTYPE),
        scratch_shapes=[pltpu.VMEM((hps, tq, LANES), F32), pltpu.VMEM((hps, tq, LANES), F32)],
        compiler_params=_cparams(("parallel", "parallel", "arbitrary")), name="mla_attn",
    )(q, k, v)


def _diff_body(q_ref, k_ref, v_ref, lam_ref, g_ref, o_ref, m_ref, acc_ref, *, tq, tk, lam_init):
    q_first = pl.program_id(2) * tq
    n_full = q_first // tk
    _init_stats(m_ref, acc_ref)
    q = q_ref[0]
    grp = lax.broadcasted_iota(jnp.int32, q.shape, 1) // DIFF_QK
    q4 = jnp.concatenate([jnp.where(grp == c, q, jnp.zeros_like(q)) for c in range(4)], axis=0)
    head_rows = [slice(0, 2 * tq), slice(2 * tq, 4 * tq)]

    def tile(j, bias):
        k0 = pl.multiple_of(j * tk, tk)
        s = _dot_nt(q4, k_ref[0, pl.ds(k0, tk), :])
        if bias is not None:
            s = s + jnp.concatenate([bias] * 4, axis=0)
        p, alpha = _tile_probs(s, m_ref)
        for hl, r in enumerate(head_rows):
            acc_ref[r, :] = alpha[r] * acc_ref[r, :] + _dot(p[r], v_ref[0, pl.ds(k0, tk), hl * LANES:(hl + 1) * LANES])

    def full_tile(j, carry):
        tile(j, None)
        return carry

    lax.fori_loop(0, n_full, full_tile, 0)
    tile(n_full, _causal_bias(q_first, n_full * tk, tq, tk))
    lf = lam_ref[...]
    lam = (jnp.exp(jnp.sum(lf[0:1] * lf[1:2], keepdims=True))
           - jnp.exp(jnp.sum(lf[2:3] * lf[3:4], keepdims=True)) + lam_init)
    a = _normalised(acc_ref[...])
    lane = lax.broadcasted_iota(jnp.int32, (tq, LANES), 1)
    normed = []
    for hl in range(2):
        o = a[2 * hl * tq:(2 * hl + 1) * tq] - lam * a[(2 * hl + 1) * tq:(2 * hl + 2) * tq]
        ms = jnp.sum(jnp.where(lane < DIFF_V, o * o, 0.0), axis=-1, keepdims=True) * (1.0 / DIFF_V)
        normed.append(o * lax.rsqrt(ms + EPS))
    o_ref[0] = (_pair_lanes(normed[0], normed[1]) * g_ref[...] * (1.0 - lam_init)).astype(o_ref.dtype)


def _diff_attention(q, k, v, diff_lambda, norm_g, lam_init, tq=512, tk=512):
    b, t, _ = q.shape
    assert t % tk == 0 and tk % tq == 0
    pairs = DIFF_HEADS // 2
    lam_pad = jnp.pad(diff_lambda.astype(F32), ((0, 4), (0, LANES - DIFF_QK)))
    g2 = jnp.tile(norm_g.astype(F32), 2).reshape(1, LANES)
    spec_qo = pl.BlockSpec((1, tq, LANES), lambda bi, p, i: (bi, i, p))
    return pl.pallas_call(
        functools.partial(_diff_body, tq=tq, tk=tk, lam_init=lam_init),
        grid=(b, pairs, t // tq),
        in_specs=[spec_qo, pl.BlockSpec((1, t, LANES), lambda bi, p, i: (bi, 0, p)),
                  pl.BlockSpec((1, t, 2 * LANES), lambda bi, p, i: (bi, 0, p)),
                  _const_spec(lam_pad.shape), _const_spec(g2.shape)],
        out_specs=spec_qo,
        out_shape=jax.ShapeDtypeStruct((b, t, pairs * LANES), MXU_DTYPE),
        scratch_shapes=[pltpu.VMEM((4 * tq, LANES), F32), pltpu.VMEM((4 * tq, LANES), F32)],
        compiler_params=_cparams(("parallel", "parallel", "arbitrary")), name="diff_attn",
    )(q, k, v, lam_pad, g2)


def _compress_body(xk_ref, xv_ref, w1_ref, w1o_ref, pos_ref, b1_ref, w2_ref, kc_ref, vc_ref):
    for kv, (x_ref, o_ref) in enumerate(((xk_ref, kc_ref), (xv_ref, vc_ref))):
        x = x_ref[0]
        posb = _dot(jnp.broadcast_to(pos_ref[kv], (8, pos_ref.shape[-1])), w1o_ref[kv])[0:1] + b1_ref[kv]
        for g in range(NSA_GROUPS):
            ab = _dot(x, w1_ref[kv, g])
            first, second = ab[:, :CMP_HIDDEN], ab[:, CMP_HIDDEN:]
            nxt = jnp.concatenate([second[1:], second[:1]], axis=0)
            hid = first + nxt + posb
            act = (hid * jax.nn.sigmoid(hid)).astype(MXU_DTYPE)
            o_ref[0, :, g * LANES:(g + 1) * LANES] = _dot(act, w2_ref[kv]).astype(o_ref.dtype)


def _compress_weights(pos, w1, b1, w2):
    z = w1.shape[0]
    half = CMP_BLOCK // 2
    assert half == CMP_STRIDE
    w1r = w1.reshape(z, 2, 2, half, NSA_D, CMP_HIDDEN)
    both = jnp.concatenate([w1r[:, :, 0], w1r[:, :, 1]], axis=-1).astype(MXU_DTYPE)
    eye = jnp.eye(NSA_GROUPS, dtype=MXU_DTYPE)
    w1g = both[:, :, None, :, None, :, :] * eye[None, None, :, None, :, None, None]
    w1g = w1g.reshape(z, 2, NSA_GROUPS, half * NSA_GROUPS * NSA_D, 2 * CMP_HIDDEN)
    w2p = jnp.pad(w2, ((0, 0), (0, 0), (0, 0), (0, LANES - NSA_D))).astype(MXU_DTYPE)
    posf = pos.reshape(z, 2, 1, CMP_BLOCK * NSA_D).astype(MXU_DTYPE)
    return w1g, w1.astype(MXU_DTYPE), posf, b1.reshape(z, 2, 1, CMP_HIDDEN).astype(F32), w2p


def _compress(kct, vct, weights):
    b, t, _ = kct.shape
    nch = t // CMP_STRIDE
    xk = kct.reshape(b, nch, CMP_STRIDE * NSA_GROUPS * NSA_D)
    xv = vct.reshape(b, nch, CMP_STRIDE * NSA_GROUPS * NSA_D)
    args = [xk, xv] + list(weights)
    x_spec = pl.BlockSpec((1, nch, xk.shape[-1]), lambda bi: (bi, 0, 0))
    o_spec = pl.BlockSpec((1, nch, NSA_GROUPS * LANES), lambda bi: (bi, 0, 0))
    return pl.pallas_call(
        _compress_body, grid=(b,),
        in_specs=[x_spec, x_spec] + [_const_spec(a.shape) for a in args[2:]],
        out_specs=[o_spec, o_spec],
        out_shape=[jax.ShapeDtypeStruct((b, nch, NSA_GROUPS * LANES), MXU_DTYPE)] * 2,
        compiler_params=_cparams(("parallel",)), name="nsa_compress",
    )(*args)


def _sel_map_t(nch):
    r_sel, r_cmp = SEL_BLOCK // CMP_STRIDE, CMP_BLOCK // CMP_STRIDE
    nc = nch - 1
    mt = np.zeros((nch * CMP_STRIDE // SEL_BLOCK, nch), np.float32)
    for j in range(mt.shape[0]):
        for m in range(r_sel):
            for n in range(r_cmp):
                idx = j * r_sel - m - n
                if 0 <= idx < nc:
                    mt[j, idx] += 1.0
    return mt


def _split3(x):
    hi = x.astype(MXU_DTYPE)
    r1 = x - hi.astype(F32)
    mid = r1.astype(MXU_DTYPE)
    lo = (r1 - mid.astype(F32)).astype(MXU_DTYPE)
    return hi, mid, lo


def _nsa_body(q_ref, kc_ref, vc_ref, ks_ref, vs_ref, kw_ref, vw_ref, gt_ref, mt_ref, o_ref,
              m_ref, acc_ref, *, nq, tk, win_rows):
    rows = HG * nq
    q_first = pl.program_id(2) * nq
    blk_first = q_first // SEL_BLOCK
    lane = lax.broadcasted_iota(jnp.int32, (rows, LANES), 1)
    qpos = q_first + lax.broadcasted_iota(jnp.int32, (nq, 1), 0)
    tpos = jnp.concatenate([qpos] * HG, axis=0)
    qe = q_ref[0, :, 0:LANES].astype(F32)
    qo = q_ref[0, :, LANES:2 * LANES].astype(F32)
    q4 = jnp.concatenate([qe, pltpu.roll(qe, NSA_D, 1), qo, pltpu.roll(qo, NSA_D, 1)], axis=0)
    is_q = lane < NSA_D

    s = _dot_nt(jnp.where(is_q, q4, 0.0).astype(MXU_DTYPE), kc_ref[0])
    n_idx = lax.broadcasted_iota(jnp.int32, s.shape, 1)
    valid = n_idx * CMP_STRIDE + (CMP_BLOCK - 1) <= tpos
    sm = jnp.where(valid, s, NEG)
    e = jnp.where(valid, jnp.exp(sm - jnp.max(sm, axis=-1, keepdims=True)), 0.0)
    den = jnp.sum(e, axis=-1, keepdims=True)
    p = e * jnp.where(den > 0.0, 1.0 / den, 0.0)
    o_cmp = _dot(p.astype(MXU_DTYPE), vc_ref[0])

    pg = p[0:nq]
    for hl in range(1, HG):
        pg = pg + p[hl * nq:(hl + 1) * nq]
    mt = mt_ref[...]
    p_slc_t = sum(_dot_nt(mt, part) for part in _split3(pg))
    nb = p_slc_t.shape[0]
    jrow = lax.broadcasted_iota(jnp.int32, (nb, nq), 0)
    cblk = blk_first + lax.broadcasted_iota(jnp.int32, (nb, nq), 1) // SEL_BLOCK
    forced = (jrow == 0) | (jrow == cblk) | (jrow == cblk - 1)
    score = jnp.where(jrow > cblk, -1.0, p_slc_t + jnp.where(forced, FORCE, 0.0))
    sub = 8
    groups = [score[r * sub:(r + 1) * sub] for r in range(nb // sub)]
    cnts = [jnp.zeros((sub, nq), F32) for _ in groups]
    jl = lax.broadcasted_iota(jnp.int32, (sub, nq), 0)
    for i in range(nb):
        row = jnp.broadcast_to(score[i:i + 1], (sub, nq))
        for r, grp in enumerate(groups):
            if r * sub > i:
                ahead = row >= grp
            elif r * sub + sub - 1 < i:
                ahead = row > grp
            else:
                ahead = (row > grp) | ((row == grp) & (jl > i - r * sub))
            cnts[r] = cnts[r] + jnp.where(ahead, 1.0, 0.0)
    cnt = jnp.concatenate(cnts, axis=0)
    sel_t = (cnt < float(SEL_TOPK)) & (jrow <= cblk)
    neg_t = jnp.where(sel_t, 0.0, NEG)
    bias = []
    for c in range(nq // LANES):
        pieces = [jnp.zeros((NSA_D, LANES), F32), neg_t[:, c * LANES:(c + 1) * LANES]]
        if LANES - NSA_D - nb:
            pieces.append(jnp.zeros((LANES - NSA_D - nb, LANES), F32))
        bias.append(jnp.concatenate(pieces, axis=0).T)
    bias = jnp.concatenate(bias, axis=0)
    q_aug = jnp.where(is_q, q4, jnp.concatenate([bias] * HG, axis=0)).astype(MXU_DTYPE)

    _init_stats(m_ref, acc_ref)
    n_full = q_first // tk

    def slc_tile(j, tile_bias):
        k0 = pl.multiple_of(j * tk, tk)
        s = _dot_nt(q_aug, ks_ref[0, pl.ds(k0, tk), :])
        if tile_bias is not None:
            s = s + jnp.concatenate([tile_bias] * HG, axis=0)
        p, alpha = _tile_probs(s, m_ref)
        acc_ref[...] = alpha * acc_ref[...] + _dot(p, vs_ref[0, pl.ds(k0, tk), :])

    def slc_full(j, carry):
        slc_tile(j, None)
        return carry

    lax.fori_loop(0, n_full, slc_full, 0)
    slc_tile(n_full, _causal_bias(q_first, n_full * tk, nq, tk))
    o_slc = _normalised(acc_ref[...])

    w0 = pl.multiple_of(jnp.maximum(q_first - WINDOW, 0), nq)
    delta = qpos - (w0 + lax.broadcasted_iota(jnp.int32, (nq, win_rows), 1))
    wbias = jnp.where((delta >= 0) & (delta < WINDOW), 0.0, NEG)
    s = _dot_nt(q_aug, kw_ref[0, pl.ds(w0, win_rows), :]) + jnp.concatenate([wbias] * HG, axis=0)
    e = jnp.exp(s - jnp.max(s, axis=-1, keepdims=True))
    o_win = _normalised(_dot(e.astype(MXU_DTYPE), vw_ref[0, pl.ds(w0, win_rows), :]))

    gt = gt_ref[0]
    glane = lax.broadcasted_iota(jnp.int32, gt.shape, 1)
    gate = lambda c: jnp.sum(jnp.where(glane == c, gt, 0.0), axis=-1, keepdims=True)
    outs = []
    for hl in range(HG):
        r = slice(hl * nq, (hl + 1) * nq)
        outs.append(gate(3 * hl) * o_cmp[r] + gate(3 * hl + 1) * o_slc[r] + gate(3 * hl + 2) * o_win[r])
    for pr in range(HG // 2):
        o_ref[0, :, pr * LANES:(pr + 1) * LANES] = _pair_lanes(outs[2 * pr], outs[2 * pr + 1]).astype(o_ref.dtype)


def _nsa_attention(q, kc, vc, ks, vs, kw, vw, gates, nq=256, tk=512):
    b, t, _ = q.shape
    nch = kc.shape[1]
    win_rows = WINDOW + nq
    assert t % tk == 0 and tk % nq == 0 and nq % LANES == 0 and t >= win_rows and HG == 4
    mt = jnp.asarray(_sel_map_t(nch), MXU_DTYPE)
    nb = mt.shape[0]
    assert nb <= LANES - NSA_D
    rows = HG * nq
    spec_q = pl.BlockSpec((1, nq, 2 * LANES), lambda bi, g, i: (bi, i, g))
    spec_c = pl.BlockSpec((1, nch, LANES), lambda bi, g, i: (bi, 0, g))
    spec_t = pl.BlockSpec((1, t, LANES), lambda bi, g, i: (bi, 0, g))
    spec_g = pl.BlockSpec((1, nq, LANES), lambda bi, g, i: (bi, i, g))
    return pl.pallas_call(
        functools.partial(_nsa_body, nq=nq, tk=tk, win_rows=win_rows),
        grid=(b, NSA_GROUPS, t // nq),
        in_specs=[spec_q, spec_c, spec_c, spec_t, spec_t, spec_t, spec_t, spec_g, _const_spec(mt.shape)],
        out_specs=spec_q,
        out_shape=jax.ShapeDtypeStruct((b, t, NSA_GROUPS * 2 * LANES), MXU_DTYPE),
        scratch_shapes=[pltpu.VMEM((rows, LANES), F32), pltpu.VMEM((rows, LANES), F32)],
        compiler_params=_cparams(("parallel", "parallel", "arbitrary")), name="nsa_attn",
    )(q, kc, vc, ks, vs, kw, vw, gates, mt)


def kernel(x, ffn_norm_g, ffn_w_gate, ffn_w_up, ffn_w_down, mix_norm_g, w_in, mla_q_norm_g, mla_w_uq, mla_kv_norm_g, mla_w_ukv, diff_lambda, diff_norm_g, nsa_cmp_pos, nsa_cmp_w1, nsa_cmp_b1, nsa_cmp_w2, nsa_gate_b, w_out, final_norm_g):
    b, t, d = x.shape
    depth = w_in.shape[0]
    tab = _rope_tables(t)
    wg, wu, wd = (w.astype(MXU_DTYPE) for w in (ffn_w_gate, ffn_w_up, ffn_w_down))
    w_cat, wuq, wukv = _mix_weights(w_in, mla_w_uq, mla_w_ukv)
    cmp_w = _compress_weights(nsa_cmp_pos, nsa_cmp_w1, nsa_cmp_b1, nsa_cmp_w2)
    wo = w_out.astype(MXU_DTYPE)
    sh = lambda a: a.reshape(b, t, a.shape[-1])
    flat = lambda a: a.reshape(b * t, a.shape[-1])
    x2 = x.reshape(b * t, d)
    for l in range(depth):
        lam_init = 0.8 - 0.6 * math.exp(-0.3 * l)
        x2 = _ffn(x2, ffn_norm_g[l, 0], wg[l, 0], wu[l, 0], wd[l, 0])
        (qm, km, vm, qd, kd, vd, qn, kct, vct, ks, vs, kw, vw, gt) = _mixproj(
            x2, t, mix_norm_g[l], w_cat[l], tab, mla_q_norm_g[l], wuq[l], mla_kv_norm_g[l], wukv[l], nsa_gate_b[l])
        o_mla = _mla_attention(sh(qm), sh(km), sh(vm))
        o_diff = _diff_attention(sh(qd), sh(kd), sh(vd), diff_lambda[l], diff_norm_g[l], lam_init)
        kc, vc = _compress(sh(kct), sh(vct), [w[l] for w in cmp_w])
        o_nsa = _nsa_attention(sh(qn), kc, vc, sh(ks), sh(vs), sh(kw), sh(vw), sh(gt))
        x2 = _ffn(x2, ffn_norm_g[l, 1], wg[l, 1], wu[l, 1], wd[l, 1],
                  mix=(flat(o_mla), flat(o_diff), flat(o_nsa), wo[l]),
                  final_g=final_norm_g if l == depth - 1 else None)
    return x2.reshape(b, t, d)
```

```python
import functools
import math

import numpy as np
import jax
import jax.numpy as jnp
from jax import lax
from jax.experimental import pallas as pl
from jax.experimental.pallas import tpu as pltpu

F32 = jnp.float32
MXU_DTYPE = jnp.bfloat16

LANES = 128
VMEM_LIMIT = 56 * 1024 * 1024

MLA_HEADS, MLA_NOPE, MLA_ROPE, MLA_V = 4, 64, 32, 64
DIFF_HEADS, DIFF_QK, DIFF_V = 4, 32, 64
NSA_HEADS, NSA_GROUPS, NSA_D = 8, 2, 64
CMP_BLOCK, CMP_STRIDE, CMP_HIDDEN = 32, 16, 256
SEL_BLOCK, SEL_TOPK, WINDOW = 64, 16, 512
ROPE_THETA, EPS, NEG, FORCE = 10000.0, 1e-6, -1e30, 1e4
HG = NSA_HEADS // NSA_GROUPS
HEAD_V = 64
assert MLA_V == DIFF_V == NSA_D == HEAD_V

P_CQ, P_CKV, P_DV, P_VCT, P_VS, P_VW, P_NG, P_END = 0, 256, 384, 896, 1024, 1280, 1536, 1792
R_KR, R_DQ, R_DK, R_NQ, R_KCT, R_KS, R_KW, R_END = 0, 128, 384, 640, 1152, 1280, 1536, 1792
T_C32, T_S32, T_C64, T_S64, T_CQ, T_SQ, T_OH, T_END = 0, 128, 256, 384, 512, 640, 768, 896


def _cparams(sem):
    return pltpu.CompilerParams(dimension_semantics=sem, vmem_limit_bytes=VMEM_LIMIT)


def _const_spec(shape):
    n = len(shape)
    return pl.BlockSpec(shape, lambda *_: (0,) * n)


def _rms(x, g):
    return x * lax.rsqrt(jnp.mean(x * x, axis=-1, keepdims=True) + EPS) * g


def _dot(a, b):
    return jnp.dot(a, b, preferred_element_type=F32)


def _dot_nt(a, b):
    return lax.dot_general(a, b, (((1,), (1,)), ((), ())), preferred_element_type=F32)


def _ffn_body(*refs, f_chunk, mixed, final):
    refs = list(refs)
    x_ref = refs.pop(0)
    x = x_ref[...]
    if mixed:
        o_refs, wo_ref = refs[:3], refs[3]
        refs = refs[4:]
        row = 0
        for o_in in o_refs:
            x = x + _dot(o_in[...], wo_ref[row:row + o_in.shape[1], :])
            row += o_in.shape[1]
    g_ref, wg_ref, wu_ref, wd_ref = refs[:4]
    gf_ref = refs[4] if final else None
    o_ref = refs[-1]
    h = _rms(x, g_ref[...]).astype(MXU_DTYPE)
    d_ff = wg_ref.shape[1]
    acc = jnp.zeros(x.shape, F32)
    for c in range(d_ff // f_chunk):
        sl = slice(c * f_chunk, (c + 1) * f_chunk)
        gate = _dot(h, wg_ref[:, sl])
        up = _dot(h, wu_ref[:, sl])
        act = (gate * jax.nn.sigmoid(gate) * up).astype(MXU_DTYPE)
        acc = acc + _dot(act, wd_ref[sl, :])
    y = x + 0.5 * acc
    if final:
        y = _rms(y, gf_ref[...])
    o_ref[...] = y


def _ffn(x2, g, wg, wu, wd, mix=None, final_g=None, tm=512, f_chunk=256):
    n, d = x2.shape
    d_ff = wg.shape[1]
    assert n % tm == 0 and d_ff % f_chunk == 0
    final, mixed = final_g is not None, mix is not None
    row = lambda w: pl.BlockSpec((tm, w), lambda i: (i, 0))
    in_specs, args = [row(d)], [x2]
    if mixed:
        *outs, w_out = mix
        in_specs += [row(o.shape[1]) for o in outs] + [_const_spec(w_out.shape)]
        args += list(outs) + [w_out]
    in_specs += [_const_spec((1, d)), _const_spec((d, d_ff)), _const_spec((d, d_ff)), _const_spec((d_ff, d))]
    args += [g.reshape(1, d), wg, wu, wd]
    if final:
        in_specs.append(_const_spec((1, d)))
        args.append(final_g.reshape(1, d))
    name = "ffn" + ("_mix" if mixed else "") + ("_final" if final else "")
    return pl.pallas_call(
        functools.partial(_ffn_body, f_chunk=f_chunk, mixed=mixed, final=final),
        grid=(n // tm,), in_specs=in_specs, out_specs=row(d),
        out_shape=jax.ShapeDtypeStruct((n, d), F32),
        compiler_params=_cparams(("parallel",)), name=name,
    )(*args)


def _pad_groups(w, width, lead=0):
    g = w.reshape(w.shape[:-1] + (-1, width))
    g = jnp.pad(g, [(0, 0)] * (g.ndim - 1) + [(lead, LANES - width - lead)])
    return g.reshape(w.shape[:-1] + (-1,))


def _mix_weights(w_in, w_uq, w_ukv):
    o = np.cumsum([0, 256, 128, 32, 256, 256, 256, 512, 768, 24]).tolist()
    cq, ckv, kr, dq, dk, dv, nq, nkv, ng = [w_in[..., o[i]:o[i + 1]] for i in range(9)]
    nkv = nkv.reshape(nkv.shape[:-1] + (3, 2, NSA_GROUPS * NSA_D))
    kct, vct = nkv[..., 0, 0, :], nkv[..., 0, 1, :]
    ks, vs = nkv[..., 1, 0, :], nkv[..., 1, 1, :]
    kw, vw = nkv[..., 2, 0, :], nkv[..., 2, 1, :]
    w_cat = jnp.concatenate(
        [cq, ckv, _pad_groups(dv, HEAD_V), vct, _pad_groups(vs, HEAD_V), _pad_groups(vw, HEAD_V),
         _pad_groups(ng, HG * 3),
         _pad_groups(kr, MLA_ROPE, lead=MLA_NOPE), dq, dk, nq, kct, _pad_groups(ks, NSA_D), _pad_groups(kw, NSA_D)],
        axis=-1).astype(MXU_DTYPE)
    assert w_cat.shape[-1] == P_END + R_END
    wuq = _pad_groups(w_uq, MLA_NOPE + MLA_ROPE).astype(MXU_DTYPE)
    ukv = w_ukv.reshape(w_ukv.shape[:-1] + (MLA_HEADS, MLA_NOPE + MLA_V))
    flat = lambda a: a.reshape(a.shape[:-2] + (-1,))
    wukv = jnp.concatenate([_pad_groups(flat(ukv[..., :MLA_NOPE]), MLA_NOPE),
                            _pad_groups(flat(ukv[..., MLA_NOPE:]), MLA_V)], axis=-1).astype(MXU_DTYPE)
    return w_cat, wuq, wukv


def _rope_tables(t):
    def cs(dim):
        inv = ROPE_THETA ** (-jnp.arange(0, dim, 2, dtype=F32) / dim)
        ang = jnp.arange(t, dtype=F32)[:, None] * inv[None, :]
        c, s = jnp.cos(ang), jnp.sin(ang)
        return jnp.concatenate([c, c], axis=1), jnp.concatenate([-s, s], axis=1)

    c32, s32 = cs(32)
    c64, s64 = cs(64)
    one, zero = jnp.ones((t, MLA_NOPE), F32), jnp.zeros((t, MLA_NOPE), F32)
    z32 = jnp.zeros((t, LANES - MLA_NOPE - MLA_ROPE), F32)
    blk = jnp.arange(t)[:, None] // SEL_BLOCK
    onehot = (blk == jnp.arange(LANES)[None, :] - NSA_D).astype(F32)
    return jnp.concatenate([jnp.tile(c32, (1, 4)), jnp.tile(s32, (1, 4)), jnp.tile(c64, (1, 2)),
                            jnp.tile(s64, (1, 2)), jnp.concatenate([one, c32, z32], axis=1),
                            jnp.concatenate([zero, s32, z32], axis=1), onehot], axis=1)


def _mixproj_body(x_ref, g_ref, w_ref, tab_ref, qg_ref, wuq_ref, kvg_ref, wukv_ref, gb_ref,
                  qm_ref, km_ref, vm_ref, qd_ref, kd_ref, vd_ref, qn_ref, kct_ref, vct_ref,
                  ks_ref, vs_ref, kw_ref, vw_ref, gt_ref):
    h = _rms(x_ref[...], g_ref[...]).astype(MXU_DTYPE)
    yp = _dot(h, w_ref[:, 0:P_END])
    yr = _dot(h, w_ref[:, P_END:P_END + R_END])
    tab = lambda o: tab_ref[:, o:o + LANES]
    lane = lax.broadcasted_iota(jnp.int32, (1, LANES), 1)
    ones_hi = (lane >= HEAD_V).astype(F32)

    def swap_halves(x, dim):
        half = dim // 2
        return jnp.where(lane % dim < half, pltpu.roll(x, LANES - half, 1), pltpu.roll(x, half, 1))

    def rope_of(x, dim, c, s):
        return x * tab(c) + swap_halves(x, dim) * tab(s)

    def rope(off, dim, c, s):
        return rope_of(yr[:, off:off + LANES], dim, c, s)

    def put(ref, j, val):
        ref[:, j * LANES:(j + 1) * LANES] = val.astype(ref.dtype)

    cqn = _rms(yp[:, P_CQ:P_CKV], qg_ref[...]).astype(MXU_DTYPE)
    qa = _dot(cqn, wuq_ref[...])
    q_scale = (MLA_NOPE + MLA_ROPE) ** -0.5
    nq_lanes = MLA_HEADS * LANES
    for hh in range(MLA_HEADS):
        put(qm_ref, hh, rope_of(qa[:, hh * LANES:(hh + 1) * LANES], MLA_ROPE, T_CQ, T_SQ) * q_scale)
    ckvn = _rms(yp[:, P_CKV:P_DV], kvg_ref[...]).astype(MXU_DTYPE)
    kv = _dot(ckvn, wukv_ref[...])
    k_rope = rope(R_KR, MLA_ROPE, T_C32, T_S32)
    for hh in range(MLA_HEADS):
        put(km_ref, hh, kv[:, hh * LANES:(hh + 1) * LANES] + k_rope)
        put(vm_ref, hh, kv[:, nq_lanes:][:, hh * LANES:(hh + 1) * LANES] + ones_hi)
    d_scale = DIFF_QK ** -0.5
    for j in range(2):
        put(qd_ref, j, rope(R_DQ + j * LANES, DIFF_QK, T_C32, T_S32) * d_scale)
        put(kd_ref, j, rope(R_DK + j * LANES, DIFF_QK, T_C32, T_S32))
    for hh in range(DIFF_HEADS):
        put(vd_ref, hh, yp[:, P_DV + hh * LANES:P_DV + (hh + 1) * LANES] + ones_hi)
    n_scale = NSA_D ** -0.5
    for j in range(4):
        put(qn_ref, j, rope(R_NQ + j * LANES, NSA_D, T_C64, T_S64) * n_scale)
    put(kct_ref, 0, rope(R_KCT, NSA_D, T_C64, T_S64))
    vct_ref[...] = yp[:, P_VCT:P_VS].astype(vct_ref.dtype)
    for j in range(NSA_GROUPS):
        put(ks_ref, j, rope(R_KS + j * LANES, NSA_D, T_C64, T_S64) + tab(T_OH))
        put(kw_ref, j, rope(R_KW + j * LANES, NSA_D, T_C64, T_S64))
        for ref, off in ((vs_ref, P_VS), (vw_ref, P_VW)):
            v1 = yp[:, off + j * LANES:off + (j + 1) * LANES] + ones_hi
            put(ref, 2 * j, v1)
            put(ref, 2 * j + 1, pltpu.roll(v1, HEAD_V, 1))
    gt_ref[...] = jax.nn.sigmoid(yp[:, P_NG:P_END] + gb_ref[...])


def _mixproj(x2, t, g, w_cat, tab, qg, wuq, kvg, wukv, gate_b, tm=256):
    n, d = x2.shape
    assert t % tm == 0
    tpb = t // tm
    widths = [512, 512, 512, 256, 256, 512, 512, 128, 128, 256, 512, 256, 512]
    out_shape = [jax.ShapeDtypeStruct((n, w), MXU_DTYPE) for w in widths]
    out_shape.append(jax.ShapeDtypeStruct((n, 2 * LANES), F32))
    out_specs = [pl.BlockSpec((tm, s.shape[1]), lambda i: (i, 0)) for s in out_shape]
    gb = _pad_groups(gate_b.reshape(1, -1), HG * 3)
    in_specs = [pl.BlockSpec((tm, d), lambda i: (i, 0)), _const_spec((1, d)), _const_spec(w_cat.shape),
                pl.BlockSpec((tm, T_END), lambda i: (i % tpb, 0)),
                _const_spec((1, qg.shape[0])), _const_spec(wuq.shape),
                _const_spec((1, kvg.shape[0])), _const_spec(wukv.shape), _const_spec(gb.shape)]
    return pl.pallas_call(
        _mixproj_body, grid=(n // tm,), in_specs=in_specs, out_specs=out_specs, out_shape=out_shape,
        compiler_params=_cparams(("parallel",)), name="mixproj",
    )(x2, g.reshape(1, d), w_cat, tab, qg.reshape(1, -1), wuq, kvg.reshape(1, -1), wukv, gb)


def _first_probs(s, m_ref):
    m = jnp.broadcast_to(jnp.max(s, axis=-1, keepdims=True), m_ref.shape)
    m_ref[...] = m
    return jnp.exp(s - jnp.concatenate([m] * (s.shape[1] // LANES), axis=1)).astype(MXU_DTYPE)


def _tile_probs(s, m_ref):
    m_old = m_ref[...]
    m_new = jnp.maximum(m_old, jnp.max(s, axis=-1, keepdims=True))
    m_ref[...] = m_new
    p = jnp.exp(s - jnp.concatenate([m_new] * (s.shape[1] // LANES), axis=1))
    return p.astype(MXU_DTYPE), jnp.exp(m_old - m_new)


def _normalised(acc):
    return acc / pltpu.roll(acc, HEAD_V, 1)


def _causal_bias_table(nq, tk):
    off = np.arange(tk // nq)[:, None, None] * nq
    row = np.arange(nq)[None, :, None]
    col = np.arange(tk)[None, None, :]
    return jnp.asarray(np.where(col <= off + row, 0.0, NEG), F32)


def _pair_lanes(o_even, o_odd):
    lane = lax.broadcasted_iota(jnp.int32, o_even.shape, 1)
    return jnp.where(lane < HEAD_V, o_even, pltpu.roll(o_odd, HEAD_V, 1))


def _mla_body(q_ref, k_ref, v_ref, cb_ref, o_ref, m_ref, acc_ref, *, tq, tk, hps):
    q_first = pl.program_id(2) * tq
    n_full = q_first // tk
    qs = [q_ref[0, :, hh * LANES:(hh + 1) * LANES] for hh in range(hps)]

    def tile(j, first):
        k0 = pl.multiple_of(j * tk, tk)
        for hh in range(hps):
            lanes = slice(hh * LANES, (hh + 1) * LANES)
            s = _dot_nt(qs[hh], k_ref[0, pl.ds(k0, tk), lanes])
            v = v_ref[0, pl.ds(k0, tk), lanes]
            if first:
                p = _first_probs(s + cb_ref[(q_first - n_full * tk) // tq], m_ref.at[hh])
                acc_ref[hh] = _dot(p, v)
            else:
                p, alpha = _tile_probs(s, m_ref.at[hh])
                acc_ref[hh] = alpha * acc_ref[hh] + _dot(p, v)

    def full_tile(j, carry):
        tile(j, False)
        return carry

    tile(n_full, True)
    lax.fori_loop(0, n_full, full_tile, 0)
    for pr in range(hps // 2):
        pair = _pair_lanes(_normalised(acc_ref[2 * pr]), _normalised(acc_ref[2 * pr + 1]))
        o_ref[0, :, pr * LANES:(pr + 1) * LANES] = pair.astype(o_ref.dtype)


def _mla_attention(q, k, v, tq=512, tk=512, hps=4):
    b, t, _ = q.shape
    assert t % tk == 0 and tk % tq == 0 and MLA_HEADS % hps == 0 and hps % 2 == 0
    spec_t = pl.BlockSpec((1, t, hps * LANES), lambda bi, p, i: (bi, 0, p))
    cb = _causal_bias_table(tq, tk)
    return pl.pallas_call(
        functools.partial(_mla_body, tq=tq, tk=tk, hps=hps),
        grid=(b, MLA_HEADS // hps, t // tq),
        in_specs=[pl.BlockSpec((1, tq, hps * LANES), lambda bi, p, i: (bi, i, p)), spec_t, spec_t,
                  _const_spec(cb.shape)],
        out_specs=pl.BlockSpec((1, tq, hps // 2 * LANES), lambda bi, p, i: (bi, i, p)),
        out_shape=jax.ShapeDtypeStruct((b, t, MLA_HEADS // 2 * LANES), MXU_DTYPE),
        scratch_shapes=[pltpu.VMEM((hps, tq, LANES), F32), pltpu.VMEM((hps, tq, LANES), F32)],
        compiler_params=_cparams(("parallel", "parallel", "arbitrary")), name="mla_attn",
    )(q, k, v, cb)


def _diff_body(q_ref, k_ref, v_ref, lam_ref, g_ref, cb_ref, o_ref, m_ref, acc_ref, *, tq, tk, lam_init):
    q_first = pl.program_id(2) * tq
    n_full = q_first // tk
    q = q_ref[0]
    grp = lax.broadcasted_iota(jnp.int32, q.shape, 1) // DIFF_QK
    q4 = jnp.concatenate([jnp.where(grp == c, q, jnp.zeros_like(q)) for c in range(4)], axis=0)
    head_rows = [slice(0, 2 * tq), slice(2 * tq, 4 * tq)]

    def tile(j, first):
        k0 = pl.multiple_of(j * tk, tk)
        s = _dot_nt(q4, k_ref[0, pl.ds(k0, tk), :])
        if first:
            p = _first_probs(s + jnp.concatenate([cb_ref[(q_first - n_full * tk) // tq]] * 4, axis=0), m_ref)
        else:
            p, alpha = _tile_probs(s, m_ref)
        for hl, r in enumerate(head_rows):
            pv = _dot(p[r], v_ref[0, pl.ds(k0, tk), hl * LANES:(hl + 1) * LANES])
            acc_ref[r, :] = pv if first else alpha[r] * acc_ref[r, :] + pv

    def full_tile(j, carry):
        tile(j, False)
        return carry

    tile(n_full, True)
    lax.fori_loop(0, n_full, full_tile, 0)
    lf = lam_ref[...]
    lam = (jnp.exp(jnp.sum(lf[0:1] * lf[1:2], keepdims=True))
           - jnp.exp(jnp.sum(lf[2:3] * lf[3:4], keepdims=True)) + lam_init)
    a = _normalised(acc_ref[...])
    lane = lax.broadcasted_iota(jnp.int32, (tq, LANES), 1)
    normed = []
    for hl in range(2):
        o = a[2 * hl * tq:(2 * hl + 1) * tq] - lam * a[(2 * hl + 1) * tq:(2 * hl + 2) * tq]
        ms = jnp.sum(jnp.where(lane < DIFF_V, o * o, 0.0), axis=-1, keepdims=True) * (1.0 / DIFF_V)
        normed.append(o * lax.rsqrt(ms + EPS))
    o_ref[0] = (_pair_lanes(normed[0], normed[1]) * g_ref[...] * (1.0 - lam_init)).astype(o_ref.dtype)


def _diff_attention(q, k, v, diff_lambda, norm_g, lam_init, tq=512, tk=512):
    b, t, _ = q.shape
    assert t % tk == 0 and tk % tq == 0
    pairs = DIFF_HEADS // 2
    lam_pad = jnp.pad(diff_lambda.astype(F32), ((0, 4), (0, LANES - DIFF_QK)))
    g2 = jnp.tile(norm_g.astype(F32), 2).reshape(1, LANES)
    spec_qo = pl.BlockSpec((1, tq, LANES), lambda bi, p, i: (bi, i, p))
    cb = _causal_bias_table(tq, tk)
    return pl.pallas_call(
        functools.partial(_diff_body, tq=tq, tk=tk, lam_init=lam_init),
        grid=(b, pairs, t // tq),
        in_specs=[spec_qo, pl.BlockSpec((1, t, LANES), lambda bi, p, i: (bi, 0, p)),
                  pl.BlockSpec((1, t, 2 * LANES), lambda bi, p, i: (bi, 0, p)),
                  _const_spec(lam_pad.shape), _const_spec(g2.shape), _const_spec(cb.shape)],
        out_specs=spec_qo,
        out_shape=jax.ShapeDtypeStruct((b, t, pairs * LANES), MXU_DTYPE),
        scratch_shapes=[pltpu.VMEM((4 * tq, LANES), F32), pltpu.VMEM((4 * tq, LANES), F32)],
        compiler_params=_cparams(("parallel", "parallel", "arbitrary")), name="diff_attn",
    )(q, k, v, lam_pad, g2, cb)


def _compress_body(xk_ref, xv_ref, w1_ref, w1o_ref, pos_ref, b1_ref, w2_ref, kc_ref, vc_ref):
    for kv, (x_ref, o_ref) in enumerate(((xk_ref, kc_ref), (xv_ref, vc_ref))):
        x = x_ref[0]
        posb = _dot(jnp.broadcast_to(pos_ref[kv], (8, pos_ref.shape[-1])), w1o_ref[kv])[0:1] + b1_ref[kv]
        for g in range(NSA_GROUPS):
            ab = _dot(x, w1_ref[kv, g])
            first, second = ab[:, :CMP_HIDDEN], ab[:, CMP_HIDDEN:]
            nxt = jnp.concatenate([second[1:], second[:1]], axis=0)
            hid = first + nxt + posb
            act = (hid * jax.nn.sigmoid(hid)).astype(MXU_DTYPE)
            out = _dot(act, w2_ref[kv]).astype(o_ref.dtype)
            if kv == 0:
                o_ref[0, :, g * LANES:(g + 1) * LANES] = out[:, :LANES]
            else:
                o_ref[0, :, 2 * g * LANES:2 * (g + 1) * LANES] = out


def _compress_weights(pos, w1, b1, w2):
    z = w1.shape[0]
    half = CMP_BLOCK // 2
    assert half == CMP_STRIDE
    w1r = w1.reshape(z, 2, 2, half, NSA_D, CMP_HIDDEN)
    both = jnp.concatenate([w1r[:, :, 0], w1r[:, :, 1]], axis=-1).astype(MXU_DTYPE)
    eye = jnp.eye(NSA_GROUPS, dtype=MXU_DTYPE)
    w1g = both[:, :, None, :, None, :, :] * eye[None, None, :, None, :, None, None]
    w1g = w1g.reshape(z, 2, NSA_GROUPS, half * NSA_GROUPS * NSA_D, 2 * CMP_HIDDEN)
    zeros = jnp.zeros(w2.shape[:-1] + (2 * (LANES - NSA_D),), w2.dtype)
    w2p = jnp.concatenate([w2, zeros, w2], axis=-1).astype(MXU_DTYPE)
    posf = pos.reshape(z, 2, 1, CMP_BLOCK * NSA_D).astype(MXU_DTYPE)
    return w1g, w1.astype(MXU_DTYPE), posf, b1.reshape(z, 2, 1, CMP_HIDDEN).astype(F32), w2p


def _compress(kct, vct, weights):
    b, t, _ = kct.shape
    nch = t // CMP_STRIDE
    xk = kct.reshape(b, nch, CMP_STRIDE * NSA_GROUPS * NSA_D)
    xv = vct.reshape(b, nch, CMP_STRIDE * NSA_GROUPS * NSA_D)
    args = [xk, xv] + list(weights)
    x_spec = pl.BlockSpec((1, nch, xk.shape[-1]), lambda bi: (bi, 0, 0))
    o_spec = lambda w: pl.BlockSpec((1, nch, w), lambda bi: (bi, 0, 0))
    widths = (NSA_GROUPS * LANES, 2 * NSA_GROUPS * LANES)
    return pl.pallas_call(
        _compress_body, grid=(b,),
        in_specs=[x_spec, x_spec] + [_const_spec(a.shape) for a in args[2:]],
        out_specs=[o_spec(w) for w in widths],
        out_shape=[jax.ShapeDtypeStruct((b, nch, w), MXU_DTYPE) for w in widths],
        compiler_params=_cparams(("parallel",)), name="nsa_compress",
    )(*args)


def _sel_map_t(nch):
    r_sel, r_cmp = SEL_BLOCK // CMP_STRIDE, CMP_BLOCK // CMP_STRIDE
    nc = nch - 1
    mt = np.zeros((nch * CMP_STRIDE // SEL_BLOCK, nch), np.float32)
    for j in range(mt.shape[0]):
        for m in range(r_sel):
            for n in range(r_cmp):
                idx = j * r_sel - m - n
                if 0 <= idx < nc:
                    mt[j, idx] += 1.0
    return mt


def _split3(x):
    hi = x.astype(MXU_DTYPE)
    r1 = x - hi.astype(F32)
    mid = r1.astype(MXU_DTYPE)
    lo = (r1 - mid.astype(F32)).astype(MXU_DTYPE)
    return hi, mid, lo


def _nsa_body(q_ref, kc_ref, vc_ref, ks_ref, vs_ref, kw_ref, vw_ref, gt_ref, mt_ref, ge_ref, cb_ref, wb_ref,
              o_ref, m_ref, acc_ref, *, nq, tk, win_rows):
    rows = HG * nq
    q_first = pl.program_id(2) * nq
    blk_first = q_first // SEL_BLOCK
    lane = lax.broadcasted_iota(jnp.int32, (rows, LANES), 1)
    qpos = q_first + lax.broadcasted_iota(jnp.int32, (nq, 1), 0)
    tpos = jnp.concatenate([qpos] * HG, axis=0)
    qe = q_ref[0, :, 0:LANES].astype(F32)
    qo = q_ref[0, :, LANES:2 * LANES].astype(F32)
    q4 = jnp.concatenate([qe, qo, pltpu.roll(qe, NSA_D, 1), pltpu.roll(qo, NSA_D, 1)], axis=0)
    is_q = lane < NSA_D
    even, odd = slice(0, 2 * nq), slice(2 * nq, 4 * nq)
    lo, hi = slice(0, LANES), slice(LANES, 2 * LANES)

    def pv(p, v_ref, r0=None, nr=None):
        v = v_ref[0] if r0 is None else v_ref[0, pl.ds(r0, nr), :]
        return jnp.concatenate([_dot(p[even], v[:, lo]), _dot(p[odd], v[:, hi])], axis=0)

    q_plain = jnp.where(is_q, q4, 0.0).astype(MXU_DTYPE)

    w0 = pl.multiple_of(jnp.maximum(q_first - WINDOW, 0), nq)
    wbias = wb_ref[jnp.minimum(q_first // nq, WINDOW // nq)]
    s = _dot_nt(q_plain, kw_ref[0, pl.ds(w0, win_rows), :]) + jnp.concatenate([wbias] * HG, axis=0)
    e = jnp.exp(s - jnp.max(s, axis=-1, keepdims=True))
    acc_win = pv(e.astype(MXU_DTYPE), vw_ref, w0, win_rows)
    g_all = sum(_dot(part, ge_ref[...]) for part in _split3(gt_ref[0]))

    s = _dot_nt(q_plain, kc_ref[0])
    n_idx = lax.broadcasted_iota(jnp.int32, s.shape, 1)
    valid = n_idx * CMP_STRIDE + (CMP_BLOCK - 1) <= tpos
    sm = jnp.where(valid, s, NEG)
    e = jnp.where(valid, jnp.exp(sm - jnp.max(sm, axis=-1, keepdims=True)), 0.0)
    den = jnp.sum(e, axis=-1, keepdims=True)
    p = e * jnp.where(den > 0.0, 1.0 / den, 0.0)
    o_cmp = pv(p.astype(MXU_DTYPE), vc_ref)

    pg = p[0:nq]
    for hl in range(1, HG):
        pg = pg + p[hl * nq:(hl + 1) * nq]
    mt = mt_ref[...]
    p_slc_t = sum(_dot_nt(mt, part) for part in _split3(pg))
    nb = p_slc_t.shape[0]
    jrow = lax.broadcasted_iota(jnp.int32, (nb, nq), 0)
    cblk = blk_first + lax.broadcasted_iota(jnp.int32, (nb, nq), 1) // SEL_BLOCK
    forced = (jrow == 0) | (jrow == cblk) | (jrow == cblk - 1)
    score = jnp.where(jrow > cblk, -1.0, p_slc_t + jnp.where(forced, FORCE, 0.0))
    sub = 8
    groups = [score[r * sub:(r + 1) * sub] for r in range(nb // sub)]
    cnts = [jnp.zeros((sub, nq), F32) for _ in groups]
    jl = lax.broadcasted_iota(jnp.int32, (sub, nq), 0)
    for i in range(nb):
        row = jnp.broadcast_to(score[i:i + 1], (sub, nq))
        for r, grp in enumerate(groups):
            if r * sub > i:
                ahead = row >= grp
            elif r * sub + sub - 1 < i:
                ahead = row > grp
            else:
                ahead = (row > grp) | ((row == grp) & (jl > i - r * sub))
            cnts[r] = cnts[r] + jnp.where(ahead, 1.0, 0.0)
    cnt = jnp.concatenate(cnts, axis=0)
    sel_t = (cnt < float(SEL_TOPK)) & (jrow <= cblk)
    neg_t = jnp.where(sel_t, 0.0, NEG)
    bias = []
    for c in range(nq // LANES):
        pieces = [jnp.zeros((NSA_D, LANES), F32), neg_t[:, c * LANES:(c + 1) * LANES]]
        if LANES - NSA_D - nb:
            pieces.append(jnp.zeros((LANES - NSA_D - nb, LANES), F32))
        bias.append(jnp.concatenate(pieces, axis=0).T)
    bias = jnp.concatenate(bias, axis=0)
    q_aug = jnp.where(is_q, q4, jnp.concatenate([bias] * HG, axis=0)).astype(MXU_DTYPE)

    n_full = q_first // tk
    k_diag = pl.multiple_of(n_full * tk, tk)
    s = _dot_nt(q_aug, ks_ref[0, pl.ds(k_diag, tk), :])
    p = _first_probs(s + jnp.concatenate([cb_ref[(q_first - n_full * tk) // nq]] * HG, axis=0), m_ref)
    acc_ref[...] = pv(p, vs_ref, k_diag, tk)

    def slc_full(j, carry):
        k0 = pl.multiple_of(j * tk, tk)
        p, alpha = _tile_probs(_dot_nt(q_aug, ks_ref[0, pl.ds(k0, tk), :]), m_ref)
        acc_ref[...] = alpha * acc_ref[...] + pv(p, vs_ref, k0, tk)
        return carry

    lax.fori_loop(0, n_full, slc_full, 0)
    acc_slc = acc_ref[...]

    first = lax.broadcasted_iota(jnp.int32, (nq, LANES), 1) < HEAD_V
    for pr in range(HG // 2):
        ev, od = slice(pr * nq, (pr + 1) * nq), slice((2 + pr) * nq, (3 + pr) * nq)
        gates = [g_all[:, (3 * pr + br) * LANES:(3 * pr + br + 1) * LANES] for br in range(3)]
        out = gates[0] * jnp.where(first, o_cmp[ev], o_cmp[od])
        for gate, acc in zip(gates[1:], (acc_slc, acc_win)):
            num = jnp.where(first, acc[ev], acc[od])
            den = pltpu.roll(jnp.where(first, acc[od], acc[ev]), HEAD_V, 1)
            out = out + gate * (num / den)
        o_ref[0, :, pr * LANES:(pr + 1) * LANES] = out.astype(o_ref.dtype)


def _gate_spread():
    e = np.zeros((LANES, (HG // 2) * 3 * LANES), np.float32)
    for hl in range(HG):
        for br in range(3):
            c0 = ((hl // 2) * 3 + br) * LANES + (hl % 2) * HEAD_V
            e[hl * 3 + br, c0:c0 + HEAD_V] = 1.0
    return e


def _window_bias_table(nq, win_rows):
    tabs = []
    for o in range(WINDOW // nq + 1):
        delta = (o * nq + np.arange(nq)[:, None]) - (max(o * nq - WINDOW, 0) + np.arange(win_rows)[None, :])
        tabs.append(np.where((delta >= 0) & (delta < WINDOW), 0.0, NEG))
    return jnp.asarray(np.stack(tabs), F32)


def _nsa_attention(q, kc, vc, ks, vs, kw, vw, gates, nq=256, tk=512):
    b, t, _ = q.shape
    nch = kc.shape[1]
    win_rows = WINDOW + nq
    assert t % tk == 0 and tk % nq == 0 and nq % LANES == 0 and WINDOW % nq == 0 and t >= win_rows and HG == 4
    mt = jnp.asarray(_sel_map_t(nch), MXU_DTYPE)
    nb = mt.shape[0]
    assert nb <= LANES - NSA_D
    consts = [mt, jnp.asarray(_gate_spread(), MXU_DTYPE), _causal_bias_table(nq, tk), _window_bias_table(nq, win_rows)]
    rows = HG * nq
    spec_q = pl.BlockSpec((1, nq, 2 * LANES), lambda bi, g, i: (bi, i, g))
    spec_kc = pl.BlockSpec((1, nch, LANES), lambda bi, g, i: (bi, 0, g))
    spec_vc = pl.BlockSpec((1, nch, 2 * LANES), lambda bi, g, i: (bi, 0, g))
    spec_k = pl.BlockSpec((1, t, LANES), lambda bi, g, i: (bi, 0, g))
    spec_v = pl.BlockSpec((1, t, 2 * LANES), lambda bi, g, i: (bi, 0, g))
    spec_g = pl.BlockSpec((1, nq, LANES), lambda bi, g, i: (bi, i, g))
    return pl.pallas_call(
        functools.partial(_nsa_body, nq=nq, tk=tk, win_rows=win_rows),
        grid=(b, NSA_GROUPS, t // nq),
        in_specs=[spec_q, spec_kc, spec_vc, spec_k, spec_v, spec_k, spec_v, spec_g] + [_const_spec(c.shape) for c in consts],
        out_specs=spec_q,
        out_shape=jax.ShapeDtypeStruct((b, t, NSA_GROUPS * 2 * LANES), MXU_DTYPE),
        scratch_shapes=[pltpu.VMEM((rows, LANES), F32), pltpu.VMEM((rows, LANES), F32)],
        compiler_params=_cparams(("parallel", "parallel", "arbitrary")), name="nsa_attn",
    )(q, kc, vc, ks, vs, kw, vw, gates, *consts)


def kernel(x, ffn_norm_g, ffn_w_gate, ffn_w_up, ffn_w_down, mix_norm_g, w_in, mla_q_norm_g, mla_w_uq, mla_kv_norm_g, mla_w_ukv, diff_lambda, diff_norm_g, nsa_cmp_pos, nsa_cmp_w1, nsa_cmp_b1, nsa_cmp_w2, nsa_gate_b, w_out, final_norm_g):
    b, t, d = x.shape
    depth = w_in.shape[0]
    tab = _rope_tables(t)
    wg, wu, wd = (w.astype(MXU_DTYPE) for w in (ffn_w_gate, ffn_w_up, ffn_w_down))
    w_cat, wuq, wukv = _mix_weights(w_in, mla_w_uq, mla_w_ukv)
    cmp_w = _compress_weights(nsa_cmp_pos, nsa_cmp_w1, nsa_cmp_b1, nsa_cmp_w2)
    wo = w_out.astype(MXU_DTYPE)
    sh = lambda a: a.reshape(b, t, a.shape[-1])
    flat = lambda a: a.reshape(b * t, a.shape[-1])
    x2 = x.reshape(b * t, d)
    for l in range(depth):
        lam_init = 0.8 - 0.6 * math.exp(-0.3 * l)
        x2 = _ffn(x2, ffn_norm_g[l, 0], wg[l, 0], wu[l, 0], wd[l, 0])
        (qm, km, vm, qd, kd, vd, qn, kct, vct, ks, vs, kw, vw, gt) = _mixproj(
            x2, t, mix_norm_g[l], w_cat[l], tab, mla_q_norm_g[l], wuq[l], mla_kv_norm_g[l], wukv[l], nsa_gate_b[l])
        o_mla = _mla_attention(sh(qm), sh(km), sh(vm))
        o_diff = _diff_attention(sh(qd), sh(kd), sh(vd), diff_lambda[l], diff_norm_g[l], lam_init)
        kc, vc = _compress(sh(kct), sh(vct), [w[l] for w in cmp_w])
        o_nsa = _nsa_attention(sh(qn), kc, vc, sh(ks), sh(vs), sh(kw), sh(vw), sh(gt))
        x2 = _ffn(x2, ffn_norm_g[l, 1], wg[l, 1], wu[l, 1], wd[l, 1],
                  mix=(flat(o_mla), flat(o_diff), flat(o_nsa), wo[l]),
                  final_g=final_norm_g if l == depth - 1 else None)
    return x2.reshape(b, t, d)
```

```python
import functools
import math

import numpy as np
import jax
import jax.numpy as jnp
from jax import lax
from jax.experimental import pallas as pl
from jax.experimental.pallas import tpu as pltpu

F32 = jnp.float32
MXU_DTYPE = jnp.bfloat16

LANES = 128
VMEM_LIMIT = 56 * 1024 * 1024

MLA_HEADS, MLA_NOPE, MLA_ROPE, MLA_V = 4, 64, 32, 64
DIFF_HEADS, DIFF_QK, DIFF_V = 4, 32, 64
NSA_HEADS, NSA_GROUPS, NSA_D = 8, 2, 64
CMP_BLOCK, CMP_STRIDE, CMP_HIDDEN = 32, 16, 256
SEL_BLOCK, SEL_TOPK, WINDOW = 64, 16, 512
ROPE_THETA, EPS, NEG, FORCE = 10000.0, 1e-6, -1e30, 1e4
HG = NSA_HEADS // NSA_GROUPS
HEAD_V = 64
assert MLA_V == DIFF_V == NSA_D == HEAD_V

P_CQ, P_CKV, P_DV, P_VCT, P_VS, P_VW, P_NG, P_END = 0, 256, 384, 896, 1024, 1280, 1536, 1792
R_KR, R_DQ, R_DK, R_NQ, R_KCT, R_KS, R_KW, R_END = 0, 128, 384, 640, 1152, 1280, 1536, 1792
T_C32, T_S32, T_C64, T_S64, T_CQ, T_SQ, T_OH, T_END = 0, 128, 256, 384, 512, 640, 768, 896


def _cparams(sem):
    return pltpu.CompilerParams(dimension_semantics=sem, vmem_limit_bytes=VMEM_LIMIT)


def _const_spec(shape):
    n = len(shape)
    return pl.BlockSpec(shape, lambda *_: (0,) * n)


def _rms(x, g):
    return x * lax.rsqrt(jnp.mean(x * x, axis=-1, keepdims=True) + EPS) * g


def _dot(a, b):
    return jnp.dot(a, b, preferred_element_type=F32)


def _dot_nt(a, b):
    return lax.dot_general(a, b, (((1,), (1,)), ((), ())), preferred_element_type=F32)


def _ffn_body(*refs, f_chunk, mixed, final):
    refs = list(refs)
    x_ref = refs.pop(0)
    x = x_ref[...]
    if mixed:
        o_refs, wo_ref = refs[:3], refs[3]
        refs = refs[4:]
        row = 0
        for o_in in o_refs:
            x = x + _dot(o_in[...], wo_ref[row:row + o_in.shape[1], :])
            row += o_in.shape[1]
    g_ref, wg_ref, wu_ref, wd_ref = refs[:4]
    gf_ref = refs[4] if final else None
    o_ref = refs[-1]
    h = _rms(x, g_ref[...]).astype(MXU_DTYPE)
    d_ff = wg_ref.shape[1]
    acc = jnp.zeros(x.shape, F32)
    for c in range(d_ff // f_chunk):
        sl = slice(c * f_chunk, (c + 1) * f_chunk)
        gate = _dot(h, wg_ref[:, sl])
        up = _dot(h, wu_ref[:, sl])
        act = (gate * jax.nn.sigmoid(gate) * up).astype(MXU_DTYPE)
        acc = acc + _dot(act, wd_ref[sl, :])
    y = x + 0.5 * acc
    if final:
        y = _rms(y, gf_ref[...])
    o_ref[...] = y


def _ffn(x2, g, wg, wu, wd, mix=None, final_g=None, tm=512, f_chunk=256):
    n, d = x2.shape
    d_ff = wg.shape[1]
    assert n % tm == 0 and d_ff % f_chunk == 0
    final, mixed = final_g is not None, mix is not None
    row = lambda w: pl.BlockSpec((tm, w), lambda i: (i, 0))
    in_specs, args = [row(d)], [x2]
    if mixed:
        *outs, w_out = mix
        in_specs += [row(o.shape[1]) for o in outs] + [_const_spec(w_out.shape)]
        args += list(outs) + [w_out]
    in_specs += [_const_spec((1, d)), _const_spec((d, d_ff)), _const_spec((d, d_ff)), _const_spec((d_ff, d))]
    args += [g.reshape(1, d), wg, wu, wd]
    if final:
        in_specs.append(_const_spec((1, d)))
        args.append(final_g.reshape(1, d))
    name = "ffn" + ("_mix" if mixed else "") + ("_final" if final else "")
    return pl.pallas_call(
        functools.partial(_ffn_body, f_chunk=f_chunk, mixed=mixed, final=final),
        grid=(n // tm,), in_specs=in_specs, out_specs=row(d),
        out_shape=jax.ShapeDtypeStruct((n, d), F32),
        compiler_params=_cparams(("parallel",)), name=name,
    )(*args)


def _pad_groups(w, width, lead=0):
    g = w.reshape(w.shape[:-1] + (-1, width))
    g = jnp.pad(g, [(0, 0)] * (g.ndim - 1) + [(lead, LANES - width - lead)])
    return g.reshape(w.shape[:-1] + (-1,))


def _mix_weights(w_in, w_uq, w_ukv):
    o = np.cumsum([0, 256, 128, 32, 256, 256, 256, 512, 768, 24]).tolist()
    cq, ckv, kr, dq, dk, dv, nq, nkv, ng = [w_in[..., o[i]:o[i + 1]] for i in range(9)]
    nkv = nkv.reshape(nkv.shape[:-1] + (3, 2, NSA_GROUPS * NSA_D))
    kct, vct = nkv[..., 0, 0, :], nkv[..., 0, 1, :]
    ks, vs = nkv[..., 1, 0, :], nkv[..., 1, 1, :]
    kw, vw = nkv[..., 2, 0, :], nkv[..., 2, 1, :]
    w_cat = jnp.concatenate(
        [cq, ckv, _pad_groups(dv, HEAD_V), vct, _pad_groups(vs, HEAD_V), _pad_groups(vw, HEAD_V),
         _pad_groups(ng, HG * 3),
         _pad_groups(kr, MLA_ROPE, lead=MLA_NOPE), dq, dk, nq, kct, _pad_groups(ks, NSA_D), _pad_groups(kw, NSA_D)],
        axis=-1).astype(MXU_DTYPE)
    assert w_cat.shape[-1] == P_END + R_END
    wuq = _pad_groups(w_uq, MLA_NOPE + MLA_ROPE).astype(MXU_DTYPE)
    ukv = w_ukv.reshape(w_ukv.shape[:-1] + (MLA_HEADS, MLA_NOPE + MLA_V))
    flat = lambda a: a.reshape(a.shape[:-2] + (-1,))
    wukv = jnp.concatenate([_pad_groups(flat(ukv[..., :MLA_NOPE]), MLA_NOPE),
                            _pad_groups(flat(ukv[..., MLA_NOPE:]), MLA_V)], axis=-1).astype(MXU_DTYPE)
    return w_cat, wuq, wukv


def _rope_tables(t):
    def cs(dim):
        inv = ROPE_THETA ** (-jnp.arange(0, dim, 2, dtype=F32) / dim)
        ang = jnp.arange(t, dtype=F32)[:, None] * inv[None, :]
        c, s = jnp.cos(ang), jnp.sin(ang)
        return jnp.concatenate([c, c], axis=1), jnp.concatenate([-s, s], axis=1)

    c32, s32 = cs(32)
    c64, s64 = cs(64)
    one, zero = jnp.ones((t, MLA_NOPE), F32), jnp.zeros((t, MLA_NOPE), F32)
    z32 = jnp.zeros((t, LANES - MLA_NOPE - MLA_ROPE), F32)
    blk = jnp.arange(t)[:, None] // SEL_BLOCK
    onehot = (blk == jnp.arange(LANES)[None, :] - NSA_D).astype(F32)
    return jnp.concatenate([jnp.tile(c32, (1, 4)), jnp.tile(s32, (1, 4)), jnp.tile(c64, (1, 2)),
                            jnp.tile(s64, (1, 2)), jnp.concatenate([one, c32, z32], axis=1),
                            jnp.concatenate([zero, s32, z32], axis=1), onehot], axis=1)


def _mixproj_body(x_ref, g_ref, w_ref, tab_ref, qg_ref, wuq_ref, kvg_ref, wukv_ref, gb_ref,
                  qm_ref, km_ref, vm_ref, qd_ref, kd_ref, vd_ref, qn_ref, kct_ref, vct_ref,
                  ks_ref, vs_ref, kw_ref, vw_ref, gt_ref):
    h = _rms(x_ref[...], g_ref[...]).astype(MXU_DTYPE)
    yp = _dot(h, w_ref[:, 0:P_END])
    yr = _dot(h, w_ref[:, P_END:P_END + R_END])
    tab = lambda o: tab_ref[:, o:o + LANES]
    lane = lax.broadcasted_iota(jnp.int32, (1, LANES), 1)
    ones_hi = (lane >= HEAD_V).astype(F32)

    def swap_halves(x, dim):
        half = dim // 2
        return jnp.where(lane % dim < half, pltpu.roll(x, LANES - half, 1), pltpu.roll(x, half, 1))

    def rope_of(x, dim, c, s):
        return x * tab(c) + swap_halves(x, dim) * tab(s)

    def rope(off, dim, c, s):
        return rope_of(yr[:, off:off + LANES], dim, c, s)

    def put(ref, j, val):
        ref[:, j * LANES:(j + 1) * LANES] = val.astype(ref.dtype)

    cqn = _rms(yp[:, P_CQ:P_CKV], qg_ref[...]).astype(MXU_DTYPE)
    qa = _dot(cqn, wuq_ref[...])
    q_scale = (MLA_NOPE + MLA_ROPE) ** -0.5
    nq_lanes = MLA_HEADS * LANES
    for hh in range(MLA_HEADS):
        put(qm_ref, hh, rope_of(qa[:, hh * LANES:(hh + 1) * LANES], MLA_ROPE, T_CQ, T_SQ) * q_scale)
    ckvn = _rms(yp[:, P_CKV:P_DV], kvg_ref[...]).astype(MXU_DTYPE)
    kv = _dot(ckvn, wukv_ref[...])
    k_rope = rope(R_KR, MLA_ROPE, T_C32, T_S32)
    for hh in range(MLA_HEADS):
        put(km_ref, hh, kv[:, hh * LANES:(hh + 1) * LANES] + k_rope)
        put(vm_ref, hh, kv[:, nq_lanes:][:, hh * LANES:(hh + 1) * LANES] + ones_hi)
    d_scale = DIFF_QK ** -0.5
    for j in range(2):
        put(qd_ref, j, rope(R_DQ + j * LANES, DIFF_QK, T_C32, T_S32) * d_scale)
        put(kd_ref, j, rope(R_DK + j * LANES, DIFF_QK, T_C32, T_S32))
    for hh in range(DIFF_HEADS):
        put(vd_ref, hh, yp[:, P_DV + hh * LANES:P_DV + (hh + 1) * LANES] + ones_hi)
    n_scale = NSA_D ** -0.5
    for j in range(4):
        put(qn_ref, j, rope(R_NQ + j * LANES, NSA_D, T_C64, T_S64) * n_scale)
    put(kct_ref, 0, rope(R_KCT, NSA_D, T_C64, T_S64))
    vct_ref[...] = yp[:, P_VCT:P_VS].astype(vct_ref.dtype)
    for j in range(NSA_GROUPS):
        put(ks_ref, j, rope(R_KS + j * LANES, NSA_D, T_C64, T_S64) + tab(T_OH))
        put(kw_ref, j, rope(R_KW + j * LANES, NSA_D, T_C64, T_S64))
        for ref, off in ((vs_ref, P_VS), (vw_ref, P_VW)):
            v1 = yp[:, off + j * LANES:off + (j + 1) * LANES] + ones_hi
            put(ref, 2 * j, v1)
            put(ref, 2 * j + 1, pltpu.roll(v1, HEAD_V, 1))
    gt_ref[...] = jax.nn.sigmoid(yp[:, P_NG:P_END] + gb_ref[...])


def _mixproj(x2, t, g, w_cat, tab, qg, wuq, kvg, wukv, gate_b, tm=512):
    n, d = x2.shape
    assert t % tm == 0
    tpb = t // tm
    widths = [512, 512, 512, 256, 256, 512, 512, 128, 128, 256, 512, 256, 512]
    out_shape = [jax.ShapeDtypeStruct((n, w), MXU_DTYPE) for w in widths]
    out_shape.append(jax.ShapeDtypeStruct((n, 2 * LANES), F32))
    out_specs = [pl.BlockSpec((tm, s.shape[1]), lambda i: (i, 0)) for s in out_shape]
    gb = _pad_groups(gate_b.reshape(1, -1), HG * 3)
    in_specs = [pl.BlockSpec((tm, d), lambda i: (i, 0)), _const_spec((1, d)), _const_spec(w_cat.shape),
                pl.BlockSpec((tm, T_END), lambda i: (i % tpb, 0)),
                _const_spec((1, qg.shape[0])), _const_spec(wuq.shape),
                _const_spec((1, kvg.shape[0])), _const_spec(wukv.shape), _const_spec(gb.shape)]
    return pl.pallas_call(
        _mixproj_body, grid=(n // tm,), in_specs=in_specs, out_specs=out_specs, out_shape=out_shape,
        compiler_params=_cparams(("parallel",)), name="mixproj",
    )(x2, g.reshape(1, d), w_cat, tab, qg.reshape(1, -1), wuq, kvg.reshape(1, -1), wukv, gb)


def _first_probs(s, m_ref):
    m = jnp.broadcast_to(jnp.max(s, axis=-1, keepdims=True), m_ref.shape)
    m_ref[...] = m
    return jnp.exp(s - jnp.concatenate([m] * (s.shape[1] // LANES), axis=1)).astype(MXU_DTYPE)


def _tile_probs(s, m_ref):
    m_old = m_ref[...]
    m_new = jnp.maximum(m_old, jnp.max(s, axis=-1, keepdims=True))
    m_ref[...] = m_new
    p = jnp.exp(s - jnp.concatenate([m_new] * (s.shape[1] // LANES), axis=1))
    return p.astype(MXU_DTYPE), jnp.exp(m_old - m_new)


def _causal_sweep(n_full, tk, tile_fn):
    def wide(j, carry):
        tile_fn(pl.multiple_of(j * (2 * tk), 2 * tk), 2 * tk)
        return carry

    def single(j, carry):
        tile_fn(pl.multiple_of((n_full - 1) * tk, tk), tk)
        return carry

    lax.fori_loop(0, n_full // 2, wide, 0)
    lax.fori_loop(0, n_full % 2, single, 0)


def _normalised(acc):
    return acc / pltpu.roll(acc, HEAD_V, 1)


def _causal_bias_table(nq, tk):
    off = np.arange(tk // nq)[:, None, None] * nq
    row = np.arange(nq)[None, :, None]
    col = np.arange(tk)[None, None, :]
    return jnp.asarray(np.where(col <= off + row, 0.0, NEG), F32)


def _pair_lanes(o_even, o_odd):
    lane = lax.broadcasted_iota(jnp.int32, o_even.shape, 1)
    return jnp.where(lane < HEAD_V, o_even, pltpu.roll(o_odd, HEAD_V, 1))


def _mla_body(q_ref, k_ref, v_ref, cb_ref, o_ref, m_ref, acc_ref, *, tq, tk, hps):
    q_first = pl.program_id(2) * tq
    n_full = q_first // tk
    qs = [q_ref[0, :, hh * LANES:(hh + 1) * LANES] for hh in range(hps)]

    def tile(k0, nk, first=False):
        for hh in range(hps):
            lanes = slice(hh * LANES, (hh + 1) * LANES)
            s = _dot_nt(qs[hh], k_ref[0, pl.ds(k0, nk), lanes])
            v = v_ref[0, pl.ds(k0, nk), lanes]
            if first:
                p = _first_probs(s + cb_ref[(q_first - n_full * tk) // tq], m_ref.at[hh])
                acc_ref[hh] = _dot(p, v)
            else:
                p, alpha = _tile_probs(s, m_ref.at[hh])
                acc_ref[hh] = alpha * acc_ref[hh] + _dot(p, v)

    tile(pl.multiple_of(n_full * tk, tk), tk, first=True)
    _causal_sweep(n_full, tk, tile)
    for pr in range(hps // 2):
        pair = _pair_lanes(_normalised(acc_ref[2 * pr]), _normalised(acc_ref[2 * pr + 1]))
        o_ref[0, :, pr * LANES:(pr + 1) * LANES] = pair.astype(o_ref.dtype)


def _mla_attention(q, k, v, tq=512, tk=512, hps=4):
    b, t, _ = q.shape
    assert t % tk == 0 and tk % tq == 0 and MLA_HEADS % hps == 0 and hps % 2 == 0
    spec_t = pl.BlockSpec((1, t, hps * LANES), lambda bi, p, i: (bi, 0, p))
    cb = _causal_bias_table(tq, tk)
    return pl.pallas_call(
        functools.partial(_mla_body, tq=tq, tk=tk, hps=hps),
        grid=(b, MLA_HEADS // hps, t // tq),
        in_specs=[pl.BlockSpec((1, tq, hps * LANES), lambda bi, p, i: (bi, i, p)), spec_t, spec_t,
                  _const_spec(cb.shape)],
        out_specs=pl.BlockSpec((1, tq, hps // 2 * LANES), lambda bi, p, i: (bi, i, p)),
        out_shape=jax.ShapeDtypeStruct((b, t, MLA_HEADS // 2 * LANES), MXU_DTYPE),
        scratch_shapes=[pltpu.VMEM((hps, tq, LANES), F32), pltpu.VMEM((hps, tq, LANES), F32)],
        compiler_params=_cparams(("parallel", "parallel", "arbitrary")), name="mla_attn",
    )(q, k, v, cb)


def _diff_body(q_ref, k_ref, v_ref, lam_ref, g_ref, cb_ref, o_ref, m_ref, acc_ref, *, tq, tk, lam_init):
    q_first = pl.program_id(2) * tq
    n_full = q_first // tk
    q = q_ref[0]
    grp = lax.broadcasted_iota(jnp.int32, q.shape, 1) // DIFF_QK
    q4 = jnp.concatenate([jnp.where(grp == c, q, jnp.zeros_like(q)) for c in range(4)], axis=0)
    head_rows = [slice(0, 2 * tq), slice(2 * tq, 4 * tq)]

    def tile(k0, nk, first=False):
        s = _dot_nt(q4, k_ref[0, pl.ds(k0, nk), :])
        if first:
            p = _first_probs(s + jnp.concatenate([cb_ref[(q_first - n_full * tk) // tq]] * 4, axis=0), m_ref)
        else:
            p, alpha = _tile_probs(s, m_ref)
        for hl, r in enumerate(head_rows):
            pv = _dot(p[r], v_ref[0, pl.ds(k0, nk), hl * LANES:(hl + 1) * LANES])
            acc_ref[r, :] = pv if first else alpha[r] * acc_ref[r, :] + pv

    tile(pl.multiple_of(n_full * tk, tk), tk, first=True)
    _causal_sweep(n_full, tk, tile)
    lf = lam_ref[...]
    lam = (jnp.exp(jnp.sum(lf[0:1] * lf[1:2], keepdims=True))
           - jnp.exp(jnp.sum(lf[2:3] * lf[3:4], keepdims=True)) + lam_init)
    a = _normalised(acc_ref[...])
    lane = lax.broadcasted_iota(jnp.int32, (tq, LANES), 1)
    normed = []
    for hl in range(2):
        o = a[2 * hl * tq:(2 * hl + 1) * tq] - lam * a[(2 * hl + 1) * tq:(2 * hl + 2) * tq]
        ms = jnp.sum(jnp.where(lane < DIFF_V, o * o, 0.0), axis=-1, keepdims=True) * (1.0 / DIFF_V)
        normed.append(o * lax.rsqrt(ms + EPS))
    o_ref[0] = (_pair_lanes(normed[0], normed[1]) * g_ref[...] * (1.0 - lam_init)).astype(o_ref.dtype)


def _diff_attention(q, k, v, diff_lambda, norm_g, lam_init, tq=512, tk=512):
    b, t, _ = q.shape
    assert t % tk == 0 and tk % tq == 0
    pairs = DIFF_HEADS // 2
    lam_pad = jnp.pad(diff_lambda.astype(F32), ((0, 4), (0, LANES - DIFF_QK)))
    g2 = jnp.tile(norm_g.astype(F32), 2).reshape(1, LANES)
    spec_qo = pl.BlockSpec((1, tq, LANES), lambda bi, p, i: (bi, i, p))
    cb = _causal_bias_table(tq, tk)
    return pl.pallas_call(
        functools.partial(_diff_body, tq=tq, tk=tk, lam_init=lam_init),
        grid=(b, pairs, t // tq),
        in_specs=[spec_qo, pl.BlockSpec((1, t, LANES), lambda bi, p, i: (bi, 0, p)),
                  pl.BlockSpec((1, t, 2 * LANES), lambda bi, p, i: (bi, 0, p)),
                  _const_spec(lam_pad.shape), _const_spec(g2.shape), _const_spec(cb.shape)],
        out_specs=spec_qo,
        out_shape=jax.ShapeDtypeStruct((b, t, pairs * LANES), MXU_DTYPE),
        scratch_shapes=[pltpu.VMEM((4 * tq, LANES), F32), pltpu.VMEM((4 * tq, LANES), F32)],
        compiler_params=_cparams(("parallel", "parallel", "arbitrary")), name="diff_attn",
    )(q, k, v, lam_pad, g2, cb)


def _compress_body(xk_ref, xv_ref, w1_ref, w1o_ref, pos_ref, b1_ref, w2_ref, kc_ref, vc_ref):
    for kv, (x_ref, o_ref) in enumerate(((xk_ref, kc_ref), (xv_ref, vc_ref))):
        x = x_ref[0]
        posb = _dot(jnp.broadcast_to(pos_ref[kv], (8, pos_ref.shape[-1])), w1o_ref[kv])[0:1] + b1_ref[kv]
        for g in range(NSA_GROUPS):
            ab = _dot(x, w1_ref[kv, g])
            first, second = ab[:, :CMP_HIDDEN], ab[:, CMP_HIDDEN:]
            nxt = jnp.concatenate([second[1:], second[:1]], axis=0)
            hid = first + nxt + posb
            act = (hid * jax.nn.sigmoid(hid)).astype(MXU_DTYPE)
            out = _dot(act, w2_ref[kv]).astype(o_ref.dtype)
            if kv == 0:
                o_ref[0, :, g * LANES:(g + 1) * LANES] = out[:, :LANES]
            else:
                o_ref[0, :, 2 * g * LANES:2 * (g + 1) * LANES] = out


def _compress_weights(pos, w1, b1, w2):
    z = w1.shape[0]
    half = CMP_BLOCK // 2
    assert half == CMP_STRIDE
    w1r = w1.reshape(z, 2, 2, half, NSA_D, CMP_HIDDEN)
    both = jnp.concatenate([w1r[:, :, 0], w1r[:, :, 1]], axis=-1).astype(MXU_DTYPE)
    eye = jnp.eye(NSA_GROUPS, dtype=MXU_DTYPE)
    w1g = both[:, :, None, :, None, :, :] * eye[None, None, :, None, :, None, None]
    w1g = w1g.reshape(z, 2, NSA_GROUPS, half * NSA_GROUPS * NSA_D, 2 * CMP_HIDDEN)
    zeros = jnp.zeros(w2.shape[:-1] + (2 * (LANES - NSA_D),), w2.dtype)
    w2p = jnp.concatenate([w2, zeros, w2], axis=-1).astype(MXU_DTYPE)
    posf = pos.reshape(z, 2, 1, CMP_BLOCK * NSA_D).astype(MXU_DTYPE)
    return w1g, w1.astype(MXU_DTYPE), posf, b1.reshape(z, 2, 1, CMP_HIDDEN).astype(F32), w2p


def _compress(kct, vct, weights):
    b, t, _ = kct.shape
    nch = t // CMP_STRIDE
    xk = kct.reshape(b, nch, CMP_STRIDE * NSA_GROUPS * NSA_D)
    xv = vct.reshape(b, nch, CMP_STRIDE * NSA_GROUPS * NSA_D)
    args = [xk, xv] + list(weights)
    x_spec = pl.BlockSpec((1, nch, xk.shape[-1]), lambda bi: (bi, 0, 0))
    o_spec = lambda w: pl.BlockSpec((1, nch, w), lambda bi: (bi, 0, 0))
    widths = (NSA_GROUPS * LANES, 2 * NSA_GROUPS * LANES)
    return pl.pallas_call(
        _compress_body, grid=(b,),
        in_specs=[x_spec, x_spec] + [_const_spec(a.shape) for a in args[2:]],
        out_specs=[o_spec(w) for w in widths],
        out_shape=[jax.ShapeDtypeStruct((b, nch, w), MXU_DTYPE) for w in widths],
        compiler_params=_cparams(("parallel",)), name="nsa_compress",
    )(*args)


def _sel_map_t(nch):
    r_sel, r_cmp = SEL_BLOCK // CMP_STRIDE, CMP_BLOCK // CMP_STRIDE
    nc = nch - 1
    mt = np.zeros((nch * CMP_STRIDE // SEL_BLOCK, nch), np.float32)
    for j in range(mt.shape[0]):
        for m in range(r_sel):
            for n in range(r_cmp):
                idx = j * r_sel - m - n
                if 0 <= idx < nc:
                    mt[j, idx] += 1.0
    return mt


def _split3(x):
    hi = x.astype(MXU_DTYPE)
    r1 = x - hi.astype(F32)
    mid = r1.astype(MXU_DTYPE)
    lo = (r1 - mid.astype(F32)).astype(MXU_DTYPE)
    return hi, mid, lo


def _nsa_body(q_ref, kc_ref, vc_ref, ks_ref, vs_ref, kw_ref, vw_ref, gt_ref, mt_ref, ge_ref, cb_ref, wb_ref,
              o_ref, m_ref, acc_ref, *, nq, tk, win_rows):
    rows = HG * nq
    q_first = pl.program_id(2) * nq
    blk_first = q_first // SEL_BLOCK
    lane = lax.broadcasted_iota(jnp.int32, (rows, LANES), 1)
    qpos = q_first + lax.broadcasted_iota(jnp.int32, (nq, 1), 0)
    tpos = jnp.concatenate([qpos] * HG, axis=0)
    qe = q_ref[0, :, 0:LANES].astype(F32)
    qo = q_ref[0, :, LANES:2 * LANES].astype(F32)
    q4 = jnp.concatenate([qe, qo, pltpu.roll(qe, NSA_D, 1), pltpu.roll(qo, NSA_D, 1)], axis=0)
    is_q = lane < NSA_D
    even, odd = slice(0, 2 * nq), slice(2 * nq, 4 * nq)
    lo, hi = slice(0, LANES), slice(LANES, 2 * LANES)

    def pv(p, v_ref, r0=None, nr=None):
        v = v_ref[0] if r0 is None else v_ref[0, pl.ds(r0, nr), :]
        return jnp.concatenate([_dot(p[even], v[:, lo]), _dot(p[odd], v[:, hi])], axis=0)

    q_plain = jnp.where(is_q, q4, 0.0).astype(MXU_DTYPE)

    w0 = pl.multiple_of(jnp.maximum(q_first - WINDOW, 0), nq)
    wbias = wb_ref[jnp.minimum(q_first // nq, WINDOW // nq)]
    s = _dot_nt(q_plain, kw_ref[0, pl.ds(w0, win_rows), :]) + jnp.concatenate([wbias] * HG, axis=0)
    e = jnp.exp(s - jnp.max(s, axis=-1, keepdims=True))
    acc_win = pv(e.astype(MXU_DTYPE), vw_ref, w0, win_rows)
    g_all = sum(_dot(part, ge_ref[...]) for part in _split3(gt_ref[0]))

    s = _dot_nt(q_plain, kc_ref[0])
    n_idx = lax.broadcasted_iota(jnp.int32, s.shape, 1)
    valid = n_idx * CMP_STRIDE + (CMP_BLOCK - 1) <= tpos
    sm = jnp.where(valid, s, NEG)
    e = jnp.where(valid, jnp.exp(sm - jnp.max(sm, axis=-1, keepdims=True)), 0.0)
    den = jnp.sum(e, axis=-1, keepdims=True)
    p = e * jnp.where(den > 0.0, 1.0 / den, 0.0)
    o_cmp = pv(p.astype(MXU_DTYPE), vc_ref)

    pg = p[0:nq]
    for hl in range(1, HG):
        pg = pg + p[hl * nq:(hl + 1) * nq]
    mt = mt_ref[...]
    p_slc_t = sum(_dot_nt(mt, part) for part in _split3(pg))
    nb = p_slc_t.shape[0]
    jrow = lax.broadcasted_iota(jnp.int32, (nb, nq), 0)
    cblk = blk_first + lax.broadcasted_iota(jnp.int32, (nb, nq), 1) // SEL_BLOCK
    forced = (jrow == 0) | (jrow == cblk) | (jrow == cblk - 1)
    score = jnp.where(jrow > cblk, -1.0, p_slc_t + jnp.where(forced, FORCE, 0.0))
    sub, piece = 8, 16
    last_blk = blk_first + nq // SEL_BLOCK - 1
    jl = lax.broadcasted_iota(jnp.int32, (sub, nq), 0)

    def count_piece(cnts, ib, jb):
        cnts = list(cnts)
        for i in range(ib * piece, (ib + 1) * piece):
            row = jnp.broadcast_to(score[i:i + 1], (sub, nq))
            for h in range(piece // sub):
                j0 = jb * piece + h * sub
                grp = score[j0:j0 + sub]
                if j0 > i:
                    ahead = row >= grp
                elif j0 + sub - 1 < i:
                    ahead = row > grp
                else:
                    ahead = (row > grp) | ((row == grp) & (jl > i - j0))
                cnts[h] = cnts[h] + jnp.where(ahead, 1.0, 0.0)
        return tuple(cnts)

    cnt = []
    for jb in range(nb // piece):
        cnts = tuple(jnp.zeros((sub, nq), F32) for _ in range(piece // sub))
        for ib in range(nb // piece):
            needed = max(ib, jb) * piece <= last_blk
            cnts = lax.cond(needed, functools.partial(count_piece, ib=ib, jb=jb), lambda c: c, cnts)
        cnt.extend(cnts)
    cnt = jnp.concatenate(cnt, axis=0)
    sel_t = (cnt < float(SEL_TOPK)) & (jrow <= cblk)
    neg_t = jnp.where(sel_t, 0.0, NEG)
    bias = []
    for c in range(nq // LANES):
        pieces = [jnp.zeros((NSA_D, LANES), F32), neg_t[:, c * LANES:(c + 1) * LANES]]
        if LANES - NSA_D - nb:
            pieces.append(jnp.zeros((LANES - NSA_D - nb, LANES), F32))
        bias.append(jnp.concatenate(pieces, axis=0).T)
    bias = jnp.concatenate(bias, axis=0)
    q_aug = jnp.where(is_q, q4, jnp.concatenate([bias] * HG, axis=0)).astype(MXU_DTYPE)

    n_full = q_first // tk
    k_diag = pl.multiple_of(n_full * tk, tk)
    s = _dot_nt(q_aug, ks_ref[0, pl.ds(k_diag, tk), :])
    p = _first_probs(s + jnp.concatenate([cb_ref[(q_first - n_full * tk) // nq]] * HG, axis=0), m_ref)
    acc_ref[...] = pv(p, vs_ref, k_diag, tk)

    def slc_tile(k0, nk):
        p, alpha = _tile_probs(_dot_nt(q_aug, ks_ref[0, pl.ds(k0, nk), :]), m_ref)
        acc_ref[...] = alpha * acc_ref[...] + pv(p, vs_ref, k0, nk)

    _causal_sweep(n_full, tk, slc_tile)
    acc_slc = acc_ref[...]

    first = lax.broadcasted_iota(jnp.int32, (nq, LANES), 1) < HEAD_V
    for pr in range(HG // 2):
        ev, od = slice(pr * nq, (pr + 1) * nq), slice((2 + pr) * nq, (3 + pr) * nq)
        gates = [g_all[:, (3 * pr + br) * LANES:(3 * pr + br + 1) * LANES] for br in range(3)]
        out = gates[0] * jnp.where(first, o_cmp[ev], o_cmp[od])
        for gate, acc in zip(gates[1:], (acc_slc, acc_win)):
            num = jnp.where(first, acc[ev], acc[od])
            den = pltpu.roll(jnp.where(first, acc[od], acc[ev]), HEAD_V, 1)
            out = out + gate * (num / den)
        o_ref[0, :, pr * LANES:(pr + 1) * LANES] = out.astype(o_ref.dtype)


def _gate_spread():
    e = np.zeros((LANES, (HG // 2) * 3 * LANES), np.float32)
    for hl in range(HG):
        for br in range(3):
            c0 = ((hl // 2) * 3 + br) * LANES + (hl % 2) * HEAD_V
            e[hl * 3 + br, c0:c0 + HEAD_V] = 1.0
    return e


def _window_bias_table(nq, win_rows):
    tabs = []
    for o in range(WINDOW // nq + 1):
        delta = (o * nq + np.arange(nq)[:, None]) - (max(o * nq - WINDOW, 0) + np.arange(win_rows)[None, :])
        tabs.append(np.where((delta >= 0) & (delta < WINDOW), 0.0, NEG))
    return jnp.asarray(np.stack(tabs), F32)


def _nsa_attention(q, kc, vc, ks, vs, kw, vw, gates, nq=256, tk=512):
    b, t, _ = q.shape
    nch = kc.shape[1]
    win_rows = WINDOW + nq
    assert t % tk == 0 and tk % nq == 0 and nq % LANES == 0 and WINDOW % nq == 0 and t >= win_rows and HG == 4
    mt = jnp.asarray(_sel_map_t(nch), MXU_DTYPE)
    nb = mt.shape[0]
    assert nb <= LANES - NSA_D
    consts = [mt, jnp.asarray(_gate_spread(), MXU_DTYPE), _causal_bias_table(nq, tk), _window_bias_table(nq, win_rows)]
    rows = HG * nq
    spec_q = pl.BlockSpec((1, nq, 2 * LANES), lambda bi, g, i: (bi, i, g))
    spec_kc = pl.BlockSpec((1, nch, LANES), lambda bi, g, i: (bi, 0, g))
    spec_vc = pl.BlockSpec((1, nch, 2 * LANES), lambda bi, g, i: (bi, 0, g))
    spec_k = pl.BlockSpec((1, t, LANES), lambda bi, g, i: (bi, 0, g))
    spec_v = pl.BlockSpec((1, t, 2 * LANES), lambda bi, g, i: (bi, 0, g))
    spec_g = pl.BlockSpec((1, nq, LANES), lambda bi, g, i: (bi, i, g))
    return pl.pallas_call(
        functools.partial(_nsa_body, nq=nq, tk=tk, win_rows=win_rows),
        grid=(b, NSA_GROUPS, t // nq),
        in_specs=[spec_q, spec_kc, spec_vc, spec_k, spec_v, spec_k, spec_v, spec_g] + [_const_spec(c.shape) for c in consts],
        out_specs=spec_q,
        out_shape=jax.ShapeDtypeStruct((b, t, NSA_GROUPS * 2 * LANES), MXU_DTYPE),
        scratch_shapes=[pltpu.VMEM((rows, LANES), F32), pltpu.VMEM((rows, LANES), F32)],
        compiler_params=_cparams(("parallel", "parallel", "arbitrary")), name="nsa_attn",
    )(q, kc, vc, ks, vs, kw, vw, gates, *consts)


def kernel(x, ffn_norm_g, ffn_w_gate, ffn_w_up, ffn_w_down, mix_norm_g, w_in, mla_q_norm_g, mla_w_uq, mla_kv_norm_g, mla_w_ukv, diff_lambda, diff_norm_g, nsa_cmp_pos, nsa_cmp_w1, nsa_cmp_b1, nsa_cmp_w2, nsa_gate_b, w_out, final_norm_g):
    b, t, d = x.shape
    depth = w_in.shape[0]
    tab = _rope_tables(t)
    wg, wu, wd = (w.astype(MXU_DTYPE) for w in (ffn_w_gate, ffn_w_up, ffn_w_down))
    w_cat, wuq, wukv = _mix_weights(w_in, mla_w_uq, mla_w_ukv)
    cmp_w = _compress_weights(nsa_cmp_pos, nsa_cmp_w1, nsa_cmp_b1, nsa_cmp_w2)
    wo = w_out.astype(MXU_DTYPE)
    sh = lambda a: a.reshape(b, t, a.shape[-1])
    flat = lambda a: a.reshape(b * t, a.shape[-1])
    x2 = x.reshape(b * t, d)
    for l in range(depth):
        lam_init = 0.8 - 0.6 * math.exp(-0.3 * l)
        x2 = _ffn(x2, ffn_norm_g[l, 0], wg[l, 0], wu[l, 0], wd[l, 0])
        (qm, km, vm, qd, kd, vd, qn, kct, vct, ks, vs, kw, vw, gt) = _mixproj(
            x2, t, mix_norm_g[l], w_cat[l], tab, mla_q_norm_g[l], wuq[l], mla_kv_norm_g[l], wukv[l], nsa_gate_b[l])
        o_mla = _mla_attention(sh(qm), sh(km), sh(vm))
        o_diff = _diff_attention(sh(qd), sh(kd), sh(vd), diff_lambda[l], diff_norm_g[l], lam_init)
        kc, vc = _compress(sh(kct), sh(vct), [w[l] for w in cmp_w])
        o_nsa = _nsa_attention(sh(qn), kc, vc, sh(ks), sh(vs), sh(kw), sh(vw), sh(gt))
        x2 = _ffn(x2, ffn_norm_g[l, 1], wg[l, 1], wu[l, 1], wd[l, 1],
                  mix=(flat(o_mla), flat(o_diff), flat(o_nsa), wo[l]),
                  final_g=final_norm_g if l == depth - 1 else None)
    return x2.reshape(b, t, d)
```

```python
import functools
import math

import numpy as np
import jax
import jax.numpy as jnp
from jax import lax
from jax.experimental import pallas as pl
from jax.experimental.pallas import tpu as pltpu

F32 = jnp.float32
MXU_DTYPE = jnp.bfloat16

LANES = 128
VMEM_LIMIT = 56 * 1024 * 1024

MLA_HEADS, MLA_NOPE, MLA_ROPE, MLA_V = 4, 64, 32, 64
DIFF_HEADS, DIFF_QK, DIFF_V = 4, 32, 64
NSA_HEADS, NSA_GROUPS, NSA_D = 8, 2, 64
CMP_BLOCK, CMP_STRIDE, CMP_HIDDEN = 32, 16, 256
SEL_BLOCK, SEL_TOPK, WINDOW = 64, 16, 512
ROPE_THETA, EPS, NEG, FORCE = 10000.0, 1e-6, -1e30, 1e4
HG = NSA_HEADS // NSA_GROUPS
HEAD_V = 64
assert MLA_V == DIFF_V == NSA_D == HEAD_V

P_CQ, P_CKV, P_DV, P_VCT, P_VS, P_VW, P_NG, P_END = 0, 256, 384, 896, 1024, 1280, 1536, 1792
R_KR, R_DQ, R_DK, R_NQ, R_KCT, R_KS, R_KW, R_END = 0, 128, 384, 640, 1152, 1280, 1536, 1792
T_C32, T_S32, T_C64, T_S64, T_CQ, T_SQ, T_OH, T_END = 0, 128, 256, 384, 512, 640, 768, 896


def _cparams(sem):
    return pltpu.CompilerParams(dimension_semantics=sem, vmem_limit_bytes=VMEM_LIMIT)


def _const_spec(shape):
    n = len(shape)
    return pl.BlockSpec(shape, lambda *_: (0,) * n)


def _layer_spec(stacked, idx):
    n = len(idx)
    zeros = (0,) * (stacked.ndim - n)
    return pl.BlockSpec((None,) * n + stacked.shape[n:], lambda *_: tuple(idx) + zeros)


def _rms(x, g):
    return x * lax.rsqrt(jnp.mean(x * x, axis=-1, keepdims=True) + EPS) * g


def _dot(a, b):
    return jnp.dot(a, b, preferred_element_type=F32)


def _dot_nt(a, b):
    return lax.dot_general(a, b, (((1,), (1,)), ((), ())), preferred_element_type=F32)


def _ffn_body(*refs, f_chunk, mixed, final):
    refs = list(refs)
    x_ref = refs.pop(0)
    x = x_ref[...]
    if mixed:
        o_refs, wo_ref = refs[:3], refs[3]
        refs = refs[4:]
        row = 0
        for o_in in o_refs:
            x = x + _dot(o_in[...], wo_ref[row:row + o_in.shape[1], :])
            row += o_in.shape[1]
    g_ref, wg_ref, wu_ref, wd_ref = refs[:4]
    gf_ref = refs[4] if final else None
    o_ref = refs[-1]
    h = _rms(x, g_ref[...]).astype(MXU_DTYPE)
    d_ff = wg_ref.shape[1]
    acc = jnp.zeros(x.shape, F32)
    for c in range(d_ff // f_chunk):
        sl = slice(c * f_chunk, (c + 1) * f_chunk)
        gate = _dot(h, wg_ref[:, sl])
        up = _dot(h, wu_ref[:, sl])
        act = (gate * jax.nn.sigmoid(gate) * up).astype(MXU_DTYPE)
        acc = acc + _dot(act, wd_ref[sl, :])
    y = x + 0.5 * acc
    if final:
        y = _rms(y, gf_ref[...])
    o_ref[...] = y


def _ffn(x2, g, wg, wu, wd, widx, mix=None, final_g=None, tm=512, f_chunk=256):
    n, d = x2.shape
    d_ff = wg.shape[-1]
    assert n % tm == 0 and d_ff % f_chunk == 0
    final, mixed = final_g is not None, mix is not None
    row = lambda w: pl.BlockSpec((tm, w), lambda i: (i, 0))
    in_specs, args = [row(d)], [x2]
    if mixed:
        *outs, w_out, layer = mix
        in_specs += [row(o.shape[1]) for o in outs] + [_layer_spec(w_out, (layer,))]
        args += list(outs) + [w_out]
    in_specs += [_const_spec((1, d))] + [_layer_spec(w, widx) for w in (wg, wu, wd)]
    args += [g.reshape(1, d), wg, wu, wd]
    if final:
        in_specs.append(_const_spec((1, d)))
        args.append(final_g.reshape(1, d))
    name = "ffn" + ("_mix" if mixed else "") + ("_final" if final else "")
    return pl.pallas_call(
        functools.partial(_ffn_body, f_chunk=f_chunk, mixed=mixed, final=final),
        grid=(n // tm,), in_specs=in_specs, out_specs=row(d),
        out_shape=jax.ShapeDtypeStruct((n, d), F32),
        compiler_params=_cparams(("parallel",)), name=name,
    )(*args)


def _pad_groups(w, width, lead=0):
    g = w.reshape(w.shape[:-1] + (-1, width))
    g = jnp.pad(g, [(0, 0)] * (g.ndim - 1) + [(lead, LANES - width - lead)])
    return g.reshape(w.shape[:-1] + (-1,))


def _mix_weights(w_in, w_uq, w_ukv):
    o = np.cumsum([0, 256, 128, 32, 256, 256, 256, 512, 768, 24]).tolist()
    cq, ckv, kr, dq, dk, dv, nq, nkv, ng = [w_in[..., o[i]:o[i + 1]] for i in range(9)]
    nkv = nkv.reshape(nkv.shape[:-1] + (3, 2, NSA_GROUPS * NSA_D))
    kct, vct = nkv[..., 0, 0, :], nkv[..., 0, 1, :]
    ks, vs = nkv[..., 1, 0, :], nkv[..., 1, 1, :]
    kw, vw = nkv[..., 2, 0, :], nkv[..., 2, 1, :]
    w_cat = jnp.concatenate(
        [cq, ckv, _pad_groups(dv, HEAD_V), vct, _pad_groups(vs, HEAD_V), _pad_groups(vw, HEAD_V),
         _pad_groups(ng, HG * 3),
         _pad_groups(kr, MLA_ROPE, lead=MLA_NOPE), dq, dk, nq, kct, _pad_groups(ks, NSA_D), _pad_groups(kw, NSA_D)],
        axis=-1).astype(MXU_DTYPE)
    assert w_cat.shape[-1] == P_END + R_END
    wuq = _pad_groups(w_uq, MLA_NOPE + MLA_ROPE).astype(MXU_DTYPE)
    ukv = w_ukv.reshape(w_ukv.shape[:-1] + (MLA_HEADS, MLA_NOPE + MLA_V))
    flat = lambda a: a.reshape(a.shape[:-2] + (-1,))
    wukv = jnp.concatenate([_pad_groups(flat(ukv[..., :MLA_NOPE]), MLA_NOPE),
                            _pad_groups(flat(ukv[..., MLA_NOPE:]), MLA_V)], axis=-1).astype(MXU_DTYPE)
    return w_cat, wuq, wukv


def _rope_tables(t):
    def cs(dim):
        inv = ROPE_THETA ** (-jnp.arange(0, dim, 2, dtype=F32) / dim)
        ang = jnp.arange(t, dtype=F32)[:, None] * inv[None, :]
        c, s = jnp.cos(ang), jnp.sin(ang)
        return jnp.concatenate([c, c], axis=1), jnp.concatenate([-s, s], axis=1)

    c32, s32 = cs(32)
    c64, s64 = cs(64)
    one, zero = jnp.ones((t, MLA_NOPE), F32), jnp.zeros((t, MLA_NOPE), F32)
    z32 = jnp.zeros((t, LANES - MLA_NOPE - MLA_ROPE), F32)
    blk = jnp.arange(t)[:, None] // SEL_BLOCK
    onehot = (blk == jnp.arange(LANES)[None, :] - NSA_D).astype(F32)
    return jnp.concatenate([jnp.tile(c32, (1, 4)), jnp.tile(s32, (1, 4)), jnp.tile(c64, (1, 2)),
                            jnp.tile(s64, (1, 2)), jnp.concatenate([one, c32, z32], axis=1),
                            jnp.concatenate([zero, s32, z32], axis=1), onehot], axis=1)


def _mixproj_body(x_ref, g_ref, w_ref, tab_ref, qg_ref, wuq_ref, kvg_ref, wukv_ref, gb_ref,
                  qm_ref, km_ref, vm_ref, qd_ref, kd_ref, vd_ref, qn_ref, kct_ref, vct_ref,
                  ks_ref, vs_ref, kw_ref, vw_ref, gt_ref):
    h = _rms(x_ref[...], g_ref[...]).astype(MXU_DTYPE)
    yp = _dot(h, w_ref[:, 0:P_END])
    yr = _dot(h, w_ref[:, P_END:P_END + R_END])
    tab = lambda o: tab_ref[:, o:o + LANES]
    lane = lax.broadcasted_iota(jnp.int32, (1, LANES), 1)
    ones_hi = (lane >= HEAD_V).astype(F32)

    def swap_halves(x, dim):
        half = dim // 2
        return jnp.where(lane % dim < half, pltpu.roll(x, LANES - half, 1), pltpu.roll(x, half, 1))

    def rope_of(x, dim, c, s):
        return x * tab(c) + swap_halves(x, dim) * tab(s)

    def rope(off, dim, c, s):
        return rope_of(yr[:, off:off + LANES], dim, c, s)

    def put(ref, j, val):
        ref[:, j * LANES:(j + 1) * LANES] = val.astype(ref.dtype)

    cqn = _rms(yp[:, P_CQ:P_CKV], qg_ref[...]).astype(MXU_DTYPE)
    qa = _dot(cqn, wuq_ref[...])
    q_scale = (MLA_NOPE + MLA_ROPE) ** -0.5
    nq_lanes = MLA_HEADS * LANES
    for hh in range(MLA_HEADS):
        put(qm_ref, hh, rope_of(qa[:, hh * LANES:(hh + 1) * LANES], MLA_ROPE, T_CQ, T_SQ) * q_scale)
    ckvn = _rms(yp[:, P_CKV:P_DV], kvg_ref[...]).astype(MXU_DTYPE)
    kv = _dot(ckvn, wukv_ref[...])
    k_rope = rope(R_KR, MLA_ROPE, T_C32, T_S32)
    for hh in range(MLA_HEADS):
        put(km_ref, hh, kv[:, hh * LANES:(hh + 1) * LANES] + k_rope)
        put(vm_ref, hh, kv[:, nq_lanes:][:, hh * LANES:(hh + 1) * LANES] + ones_hi)
    d_scale = DIFF_QK ** -0.5
    for j in range(2):
        put(qd_ref, j, rope(R_DQ + j * LANES, DIFF_QK, T_C32, T_S32) * d_scale)
        put(kd_ref, j, rope(R_DK + j * LANES, DIFF_QK, T_C32, T_S32))
    for hh in range(DIFF_HEADS):
        put(vd_ref, hh, yp[:, P_DV + hh * LANES:P_DV + (hh + 1) * LANES] + ones_hi)
    n_scale = NSA_D ** -0.5
    for j in range(4):
        put(qn_ref, j, rope(R_NQ + j * LANES, NSA_D, T_C64, T_S64) * n_scale)
    put(kct_ref, 0, rope(R_KCT, NSA_D, T_C64, T_S64))
    vct_ref[...] = yp[:, P_VCT:P_VS].astype(vct_ref.dtype)
    for j in range(NSA_GROUPS):
        put(ks_ref, j, rope(R_KS + j * LANES, NSA_D, T_C64, T_S64) + tab(T_OH))
        put(kw_ref, j, rope(R_KW + j * LANES, NSA_D, T_C64, T_S64))
        for ref, off in ((vs_ref, P_VS), (vw_ref, P_VW)):
            v1 = yp[:, off + j * LANES:off + (j + 1) * LANES] + ones_hi
            put(ref, 2 * j, v1)
            put(ref, 2 * j + 1, pltpu.roll(v1, HEAD_V, 1))
    gt_ref[...] = jax.nn.sigmoid(yp[:, P_NG:P_END] + gb_ref[...])


def _mixproj(x2, t, g, w_cat, tab, qg, wuq, kvg, wukv, gate_b, layer, tm=512):
    n, d = x2.shape
    assert t % tm == 0
    tpb = t // tm
    widths = [512, 512, 512, 256, 256, 512, 512, 128, 128, 256, 512, 256, 512]
    out_shape = [jax.ShapeDtypeStruct((n, w), MXU_DTYPE) for w in widths]
    out_shape.append(jax.ShapeDtypeStruct((n, 2 * LANES), F32))
    out_specs = [pl.BlockSpec((tm, s.shape[1]), lambda i: (i, 0)) for s in out_shape]
    gb = _pad_groups(gate_b.reshape(1, -1), HG * 3)
    in_specs = [pl.BlockSpec((tm, d), lambda i: (i, 0)), _const_spec((1, d)), _layer_spec(w_cat, (layer,)),
                pl.BlockSpec((tm, T_END), lambda i: (i % tpb, 0)),
                _const_spec((1, qg.shape[0])), _layer_spec(wuq, (layer,)),
                _const_spec((1, kvg.shape[0])), _layer_spec(wukv, (layer,)), _const_spec(gb.shape)]
    return pl.pallas_call(
        _mixproj_body, grid=(n // tm,), in_specs=in_specs, out_specs=out_specs, out_shape=out_shape,
        compiler_params=_cparams(("parallel",)), name="mixproj",
    )(x2, g.reshape(1, d), w_cat, tab, qg.reshape(1, -1), wuq, kvg.reshape(1, -1), wukv, gb)


def _first_probs(s, m_ref):
    m = jnp.broadcast_to(jnp.max(s, axis=-1, keepdims=True), m_ref.shape)
    m_ref[...] = m
    return jnp.exp(s - jnp.concatenate([m] * (s.shape[1] // LANES), axis=1)).astype(MXU_DTYPE)


def _tile_probs(s, m_ref):
    m_old = m_ref[...]
    m_new = jnp.maximum(m_old, jnp.max(s, axis=-1, keepdims=True))
    m_ref[...] = m_new
    p = jnp.exp(s - jnp.concatenate([m_new] * (s.shape[1] // LANES), axis=1))
    return p.astype(MXU_DTYPE), jnp.exp(m_old - m_new)


def _causal_sweep(n_full, tk, tile_fn):
    def wide(j, carry):
        tile_fn(pl.multiple_of(j * (2 * tk), 2 * tk), 2 * tk)
        return carry

    def single(j, carry):
        tile_fn(pl.multiple_of((n_full - 1) * tk, tk), tk)
        return carry

    lax.fori_loop(0, n_full // 2, wide, 0)
    lax.fori_loop(0, n_full % 2, single, 0)


def _normalised(acc):
    return acc / pltpu.roll(acc, HEAD_V, 1)


def _causal_bias_table(nq, tk):
    off = np.arange(tk // nq)[:, None, None] * nq
    row = np.arange(nq)[None, :, None]
    col = np.arange(tk)[None, None, :]
    return jnp.asarray(np.where(col <= off + row, 0.0, NEG), F32)


def _pair_lanes(o_even, o_odd):
    lane = lax.broadcasted_iota(jnp.int32, o_even.shape, 1)
    return jnp.where(lane < HEAD_V, o_even, pltpu.roll(o_odd, HEAD_V, 1))


def _mla_body(q_ref, k_ref, v_ref, cb_ref, o_ref, m_ref, acc_ref, *, tq, tk, hps):
    q_first = pl.program_id(2) * tq
    n_full = q_first // tk
    qs = [q_ref[0, :, hh * LANES:(hh + 1) * LANES] for hh in range(hps)]

    def tile(k0, nk, first=False):
        for hh in range(hps):
            lanes = slice(hh * LANES, (hh + 1) * LANES)
            s = _dot_nt(qs[hh], k_ref[0, pl.ds(k0, nk), lanes])
            v = v_ref[0, pl.ds(k0, nk), lanes]
            if first:
                p = _first_probs(s + cb_ref[(q_first - n_full * tk) // tq], m_ref.at[hh])
                acc_ref[hh] = _dot(p, v)
            else:
                p, alpha = _tile_probs(s, m_ref.at[hh])
                acc_ref[hh] = alpha * acc_ref[hh] + _dot(p, v)

    tile(pl.multiple_of(n_full * tk, tk), tk, first=True)
    _causal_sweep(n_full, tk, tile)
    for pr in range(hps // 2):
        pair = _pair_lanes(_normalised(acc_ref[2 * pr]), _normalised(acc_ref[2 * pr + 1]))
        o_ref[0, :, pr * LANES:(pr + 1) * LANES] = pair.astype(o_ref.dtype)


def _mla_attention(q, k, v, tq=512, tk=512, hps=4):
    b, t, _ = q.shape
    assert t % tk == 0 and tk % tq == 0 and MLA_HEADS % hps == 0 and hps % 2 == 0
    spec_t = pl.BlockSpec((1, t, hps * LANES), lambda bi, p, i: (bi, 0, p))
    cb = _causal_bias_table(tq, tk)
    return pl.pallas_call(
        functools.partial(_mla_body, tq=tq, tk=tk, hps=hps),
        grid=(b, MLA_HEADS // hps, t // tq),
        in_specs=[pl.BlockSpec((1, tq, hps * LANES), lambda bi, p, i: (bi, i, p)), spec_t, spec_t,
                  _const_spec(cb.shape)],
        out_specs=pl.BlockSpec((1, tq, hps // 2 * LANES), lambda bi, p, i: (bi, i, p)),
        out_shape=jax.ShapeDtypeStruct((b, t, MLA_HEADS // 2 * LANES), MXU_DTYPE),
        scratch_shapes=[pltpu.VMEM((hps, tq, LANES), F32), pltpu.VMEM((hps, tq, LANES), F32)],
        compiler_params=_cparams(("parallel", "parallel", "arbitrary")), name="mla_attn",
    )(q, k, v, cb)


def _diff_body(q_ref, k_ref, v_ref, lam_ref, g_ref, cb_ref, o_ref, m_ref, acc_ref, *, tq, tk, lam_init):
    q_first = pl.program_id(2) * tq
    n_full = q_first // tk
    q = q_ref[0]
    grp = lax.broadcasted_iota(jnp.int32, q.shape, 1) // DIFF_QK
    q4 = jnp.concatenate([jnp.where(grp == c, q, jnp.zeros_like(q)) for c in range(4)], axis=0)
    head_rows = [slice(0, 2 * tq), slice(2 * tq, 4 * tq)]

    def tile(k0, nk, first=False):
        s = _dot_nt(q4, k_ref[0, pl.ds(k0, nk), :])
        if first:
            p = _first_probs(s + jnp.concatenate([cb_ref[(q_first - n_full * tk) // tq]] * 4, axis=0), m_ref)
        else:
            p, alpha = _tile_probs(s, m_ref)
        for hl, r in enumerate(head_rows):
            pv = _dot(p[r], v_ref[0, pl.ds(k0, nk), hl * LANES:(hl + 1) * LANES])
            acc_ref[r, :] = pv if first else alpha[r] * acc_ref[r, :] + pv

    tile(pl.multiple_of(n_full * tk, tk), tk, first=True)
    _causal_sweep(n_full, tk, tile)
    lf = lam_ref[...]
    lam = (jnp.exp(jnp.sum(lf[0:1] * lf[1:2], keepdims=True))
           - jnp.exp(jnp.sum(lf[2:3] * lf[3:4], keepdims=True)) + lam_init)
    a = _normalised(acc_ref[...])
    lane = lax.broadcasted_iota(jnp.int32, (tq, LANES), 1)
    normed = []
    for hl in range(2):
        o = a[2 * hl * tq:(2 * hl + 1) * tq] - lam * a[(2 * hl + 1) * tq:(2 * hl + 2) * tq]
        ms = jnp.sum(jnp.where(lane < DIFF_V, o * o, 0.0), axis=-1, keepdims=True) * (1.0 / DIFF_V)
        normed.append(o * lax.rsqrt(ms + EPS))
    o_ref[0] = (_pair_lanes(normed[0], normed[1]) * g_ref[...] * (1.0 - lam_init)).astype(o_ref.dtype)


def _diff_attention(q, k, v, diff_lambda, norm_g, lam_init, tq=512, tk=512):
    b, t, _ = q.shape
    assert t % tk == 0 and tk % tq == 0
    pairs = DIFF_HEADS // 2
    lam_pad = jnp.pad(diff_lambda.astype(F32), ((0, 4), (0, LANES - DIFF_QK)))
    g2 = jnp.tile(norm_g.astype(F32), 2).reshape(1, LANES)
    spec_qo = pl.BlockSpec((1, tq, LANES), lambda bi, p, i: (bi, i, p))
    cb = _causal_bias_table(tq, tk)
    return pl.pallas_call(
        functools.partial(_diff_body, tq=tq, tk=tk, lam_init=lam_init),
        grid=(b, pairs, t // tq),
        in_specs=[spec_qo, pl.BlockSpec((1, t, LANES), lambda bi, p, i: (bi, 0, p)),
                  pl.BlockSpec((1, t, 2 * LANES), lambda bi, p, i: (bi, 0, p)),
                  _const_spec(lam_pad.shape), _const_spec(g2.shape), _const_spec(cb.shape)],
        out_specs=spec_qo,
        out_shape=jax.ShapeDtypeStruct((b, t, pairs * LANES), MXU_DTYPE),
        scratch_shapes=[pltpu.VMEM((4 * tq, LANES), F32), pltpu.VMEM((4 * tq, LANES), F32)],
        compiler_params=_cparams(("parallel", "parallel", "arbitrary")), name="diff_attn",
    )(q, k, v, lam_pad, g2, cb)


def _compress_body(xk_ref, xv_ref, w1_ref, w1o_ref, pos_ref, b1_ref, w2_ref, kc_ref, vc_ref):
    for kv, (x_ref, o_ref) in enumerate(((xk_ref, kc_ref), (xv_ref, vc_ref))):
        x = x_ref[0]
        posb = _dot(jnp.broadcast_to(pos_ref[kv], (8, pos_ref.shape[-1])), w1o_ref[kv])[0:1] + b1_ref[kv]
        for g in range(NSA_GROUPS):
            ab = _dot(x, w1_ref[kv, g])
            first, second = ab[:, :CMP_HIDDEN], ab[:, CMP_HIDDEN:]
            nxt = jnp.concatenate([second[1:], second[:1]], axis=0)
            hid = first + nxt + posb
            act = (hid * jax.nn.sigmoid(hid)).astype(MXU_DTYPE)
            out = _dot(act, w2_ref[kv]).astype(o_ref.dtype)
            if kv == 0:
                o_ref[0, :, g * LANES:(g + 1) * LANES] = out[:, :LANES]
            else:
                o_ref[0, :, 2 * g * LANES:2 * (g + 1) * LANES] = out


def _compress_weights(pos, w1, b1, w2):
    z = w1.shape[0]
    half = CMP_BLOCK // 2
    assert half == CMP_STRIDE
    w1r = w1.reshape(z, 2, 2, half, NSA_D, CMP_HIDDEN)
    both = jnp.concatenate([w1r[:, :, 0], w1r[:, :, 1]], axis=-1).astype(MXU_DTYPE)
    eye = jnp.eye(NSA_GROUPS, dtype=MXU_DTYPE)
    w1g = both[:, :, None, :, None, :, :] * eye[None, None, :, None, :, None, None]
    w1g = w1g.reshape(z, 2, NSA_GROUPS, half * NSA_GROUPS * NSA_D, 2 * CMP_HIDDEN)
    zeros = jnp.zeros(w2.shape[:-1] + (2 * (LANES - NSA_D),), w2.dtype)
    w2p = jnp.concatenate([w2, zeros, w2], axis=-1).astype(MXU_DTYPE)
    posf = pos.reshape(z, 2, 1, CMP_BLOCK * NSA_D).astype(MXU_DTYPE)
    return w1g, w1.astype(MXU_DTYPE), posf, b1.reshape(z, 2, 1, CMP_HIDDEN).astype(F32), w2p


def _compress(kct, vct, weights, layer):
    b, t, _ = kct.shape
    nch = t // CMP_STRIDE
    xk = kct.reshape(b, nch, CMP_STRIDE * NSA_GROUPS * NSA_D)
    xv = vct.reshape(b, nch, CMP_STRIDE * NSA_GROUPS * NSA_D)
    args = [xk, xv] + list(weights)
    x_spec = pl.BlockSpec((1, nch, xk.shape[-1]), lambda bi: (bi, 0, 0))
    o_spec = lambda w: pl.BlockSpec((1, nch, w), lambda bi: (bi, 0, 0))
    widths = (NSA_GROUPS * LANES, 2 * NSA_GROUPS * LANES)
    return pl.pallas_call(
        _compress_body, grid=(b,),
        in_specs=[x_spec, x_spec] + [_layer_spec(a, (layer,)) for a in args[2:]],
        out_specs=[o_spec(w) for w in widths],
        out_shape=[jax.ShapeDtypeStruct((b, nch, w), MXU_DTYPE) for w in widths],
        compiler_params=_cparams(("parallel",)), name="nsa_compress",
    )(*args)


def _sel_map_t(nch):
    r_sel, r_cmp = SEL_BLOCK // CMP_STRIDE, CMP_BLOCK // CMP_STRIDE
    nc = nch - 1
    mt = np.zeros((nch * CMP_STRIDE // SEL_BLOCK, nch), np.float32)
    for j in range(mt.shape[0]):
        for m in range(r_sel):
            for n in range(r_cmp):
                idx = j * r_sel - m - n
                if 0 <= idx < nc:
                    mt[j, idx] += 1.0
    return mt


def _split3(x):
    hi = x.astype(MXU_DTYPE)
    r1 = x - hi.astype(F32)
    mid = r1.astype(MXU_DTYPE)
    lo = (r1 - mid.astype(F32)).astype(MXU_DTYPE)
    return hi, mid, lo


def _nsa_body(q_ref, kc_ref, vc_ref, ks_ref, vs_ref, kw_ref, vw_ref, gt_ref, mt_ref, ge_ref, cb_ref, wb_ref,
              o_ref, m_ref, acc_ref, *, nq, tk, win_rows, ng):
    rows = HG * nq
    q_first = pl.program_id(2) * nq
    blk_first = q_first // SEL_BLOCK
    nb = mt_ref.shape[0]
    lane = lax.broadcasted_iota(jnp.int32, (rows, LANES), 1)
    qpos = q_first + lax.broadcasted_iota(jnp.int32, (nq, 1), 0)
    tpos = jnp.concatenate([qpos] * HG, axis=0)
    is_q = lane < NSA_D
    even, odd = slice(0, 2 * nq), slice(2 * nq, 4 * nq)
    jrow = lax.broadcasted_iota(jnp.int32, (nb, nq), 0)
    cblk = blk_first + lax.broadcasted_iota(jnp.int32, (nb, nq), 1) // SEL_BLOCK
    tiles = lambda g, w, j=0: slice((g * w + j) * LANES, (g * w + j + 1) * LANES)

    def pv(p, v_ref, g, r0, nr):
        return jnp.concatenate([_dot(p[even], v_ref[0, pl.ds(r0, nr), tiles(g, 2, 0)]),
                                _dot(p[odd], v_ref[0, pl.ds(r0, nr), tiles(g, 2, 1)])], axis=0)

    def before_rank(g):
        qe = q_ref[0, :, tiles(g, 2, 0)].astype(F32)
        qo = q_ref[0, :, tiles(g, 2, 1)].astype(F32)
        q4 = jnp.concatenate([qe, qo, pltpu.roll(qe, NSA_D, 1), pltpu.roll(qo, NSA_D, 1)], axis=0)
        q_plain = jnp.where(is_q, q4, 0.0).astype(MXU_DTYPE)

        w0 = pl.multiple_of(jnp.maximum(q_first - WINDOW, 0), nq)
        wbias = wb_ref[jnp.minimum(q_first // nq, WINDOW // nq)]
        s = _dot_nt(q_plain, kw_ref[0, pl.ds(w0, win_rows), tiles(g, 1)]) + jnp.concatenate([wbias] * HG, axis=0)
        e = jnp.exp(s - jnp.max(s, axis=-1, keepdims=True))
        acc_win = pv(e.astype(MXU_DTYPE), vw_ref, g, w0, win_rows)
        g_all = sum(_dot(part, ge_ref[...]) for part in _split3(gt_ref[0, :, tiles(g, 1)]))

        s = _dot_nt(q_plain, kc_ref[0, :, tiles(g, 1)])
        n_idx = lax.broadcasted_iota(jnp.int32, s.shape, 1)
        valid = n_idx * CMP_STRIDE + (CMP_BLOCK - 1) <= tpos
        sm = jnp.where(valid, s, NEG)
        e = jnp.where(valid, jnp.exp(sm - jnp.max(sm, axis=-1, keepdims=True)), 0.0)
        den = jnp.sum(e, axis=-1, keepdims=True)
        p = e * jnp.where(den > 0.0, 1.0 / den, 0.0)
        o_cmp = pv(p.astype(MXU_DTYPE), vc_ref, g, 0, vc_ref.shape[1])

        pg = p[0:nq]
        for hl in range(1, HG):
            pg = pg + p[hl * nq:(hl + 1) * nq]
        p_slc_t = sum(_dot_nt(mt_ref[...], part) for part in _split3(pg))
        forced = (jrow == 0) | (jrow == cblk) | (jrow == cblk - 1)
        score = jnp.where(jrow > cblk, -1.0, p_slc_t + jnp.where(forced, FORCE, 0.0))
        return dict(q4=q4, acc_win=acc_win, g_all=g_all, o_cmp=o_cmp, score=score)

    st = [before_rank(g) for g in range(ng)]
    sub, piece = 8, 16
    last_blk = blk_first + nq // SEL_BLOCK - 1
    jl = lax.broadcasted_iota(jnp.int32, (sub, nq), 0)

    n_cnt = piece // sub

    def count_piece(cnts, ib, jb):
        cnts = list(cnts)
        for g in range(ng):
            score = st[g]["score"]
            for i in range(ib * piece, (ib + 1) * piece):
                row = jnp.broadcast_to(score[i:i + 1], (sub, nq))
                for h in range(n_cnt):
                    j0 = jb * piece + h * sub
                    grp = score[j0:j0 + sub]
                    if j0 > i:
                        ahead = row >= grp
                    elif j0 + sub - 1 < i:
                        ahead = row > grp
                    else:
                        ahead = (row > grp) | ((row == grp) & (jl > i - j0))
                    cnts[g * n_cnt + h] = cnts[g * n_cnt + h] + jnp.where(ahead, 1.0, 0.0)
        return tuple(cnts)

    cnt = [[] for _ in range(ng)]
    for jb in range(nb // piece):
        cnts = tuple(jnp.zeros((sub, nq), F32) for _ in range(ng * n_cnt))
        for ib in range(nb // piece):
            needed = max(ib, jb) * piece <= last_blk
            cnts = lax.cond(needed, functools.partial(count_piece, ib=ib, jb=jb), lambda c: c, cnts)
        for g in range(ng):
            cnt[g].extend(cnts[g * n_cnt:(g + 1) * n_cnt])

    n_full = q_first // tk
    k_diag = pl.multiple_of(n_full * tk, tk)

    def after_rank(g):
        sel_t = (jnp.concatenate(cnt[g], axis=0) < float(SEL_TOPK)) & (jrow <= cblk)
        neg_t = jnp.where(sel_t, 0.0, NEG)
        bias = []
        for c in range(nq // LANES):
            pieces = [jnp.zeros((NSA_D, LANES), F32), neg_t[:, c * LANES:(c + 1) * LANES]]
            if LANES - NSA_D - nb:
                pieces.append(jnp.zeros((LANES - NSA_D - nb, LANES), F32))
            bias.append(jnp.concatenate(pieces, axis=0).T)
        bias = jnp.concatenate(bias, axis=0)
        q_aug = jnp.where(is_q, st[g]["q4"], jnp.concatenate([bias] * HG, axis=0)).astype(MXU_DTYPE)
        s = _dot_nt(q_aug, ks_ref[0, pl.ds(k_diag, tk), tiles(g, 1)])
        p = _first_probs(s + jnp.concatenate([cb_ref[(q_first - n_full * tk) // nq]] * HG, axis=0), m_ref.at[g])
        acc_ref[g] = pv(p, vs_ref, g, k_diag, tk)
        return q_aug

    q_augs = [after_rank(g) for g in range(ng)]

    def slc_tile(k0, nk):
        for g in range(ng):
            p, alpha = _tile_probs(_dot_nt(q_augs[g], ks_ref[0, pl.ds(k0, nk), tiles(g, 1)]), m_ref.at[g])
            acc_ref[g] = alpha * acc_ref[g] + pv(p, vs_ref, g, k0, nk)

    _causal_sweep(n_full, tk, slc_tile)

    first = lax.broadcasted_iota(jnp.int32, (nq, LANES), 1) < HEAD_V
    for g in range(ng):
        g_all, o_cmp = st[g]["g_all"], st[g]["o_cmp"]
        for pr in range(HG // 2):
            ev, od = slice(pr * nq, (pr + 1) * nq), slice((2 + pr) * nq, (3 + pr) * nq)
            gates = [g_all[:, (3 * pr + br) * LANES:(3 * pr + br + 1) * LANES] for br in range(3)]
            out = gates[0] * jnp.where(first, o_cmp[ev], o_cmp[od])
            for gate, acc in zip(gates[1:], (acc_ref[g], st[g]["acc_win"])):
                num = jnp.where(first, acc[ev], acc[od])
                den = pltpu.roll(jnp.where(first, acc[od], acc[ev]), HEAD_V, 1)
                out = out + gate * (num / den)
            o_ref[0, :, tiles(g, 2, pr)] = out.astype(o_ref.dtype)


def _gate_spread():
    e = np.zeros((LANES, (HG // 2) * 3 * LANES), np.float32)
    for hl in range(HG):
        for br in range(3):
            c0 = ((hl // 2) * 3 + br) * LANES + (hl % 2) * HEAD_V
            e[hl * 3 + br, c0:c0 + HEAD_V] = 1.0
    return e


def _window_bias_table(nq, win_rows):
    tabs = []
    for o in range(WINDOW // nq + 1):
        delta = (o * nq + np.arange(nq)[:, None]) - (max(o * nq - WINDOW, 0) + np.arange(win_rows)[None, :])
        tabs.append(np.where((delta >= 0) & (delta < WINDOW), 0.0, NEG))
    return jnp.asarray(np.stack(tabs), F32)


def _nsa_attention(q, kc, vc, ks, vs, kw, vw, gates, nq=256, tk=512, ng=2):
    b, t, _ = q.shape
    nch = kc.shape[1]
    win_rows = WINDOW + nq
    assert t % tk == 0 and tk % nq == 0 and nq % LANES == 0 and WINDOW % nq == 0 and t >= win_rows and HG == 4
    assert NSA_GROUPS % ng == 0
    mt = jnp.asarray(_sel_map_t(nch), MXU_DTYPE)
    nb = mt.shape[0]
    assert nb <= LANES - NSA_D
    consts = [mt, jnp.asarray(_gate_spread(), MXU_DTYPE), _causal_bias_table(nq, tk), _window_bias_table(nq, win_rows)]
    rows = HG * nq
    spec_q = pl.BlockSpec((1, nq, ng * 2 * LANES), lambda bi, g, i: (bi, i, g))
    spec_kc = pl.BlockSpec((1, nch, ng * LANES), lambda bi, g, i: (bi, 0, g))
    spec_vc = pl.BlockSpec((1, nch, ng * 2 * LANES), lambda bi, g, i: (bi, 0, g))
    spec_k = pl.BlockSpec((1, t, ng * LANES), lambda bi, g, i: (bi, 0, g))
    spec_v = pl.BlockSpec((1, t, ng * 2 * LANES), lambda bi, g, i: (bi, 0, g))
    spec_g = pl.BlockSpec((1, nq, ng * LANES), lambda bi, g, i: (bi, i, g))
    return pl.pallas_call(
        functools.partial(_nsa_body, nq=nq, tk=tk, win_rows=win_rows, ng=ng),
        grid=(b, NSA_GROUPS // ng, t // nq),
        in_specs=[spec_q, spec_kc, spec_vc, spec_k, spec_v, spec_k, spec_v, spec_g] + [_const_spec(c.shape) for c in consts],
        out_specs=spec_q,
        out_shape=jax.ShapeDtypeStruct((b, t, NSA_GROUPS * 2 * LANES), MXU_DTYPE),
        scratch_shapes=[pltpu.VMEM((ng, rows, LANES), F32), pltpu.VMEM((ng, rows, LANES), F32)],
        compiler_params=_cparams(("parallel", "parallel", "arbitrary")), name="nsa_attn",
    )(q, kc, vc, ks, vs, kw, vw, gates, *consts)


def kernel(x, ffn_norm_g, ffn_w_gate, ffn_w_up, ffn_w_down, mix_norm_g, w_in, mla_q_norm_g, mla_w_uq, mla_kv_norm_g, mla_w_ukv, diff_lambda, diff_norm_g, nsa_cmp_pos, nsa_cmp_w1, nsa_cmp_b1, nsa_cmp_w2, nsa_gate_b, w_out, final_norm_g):
    b, t, d = x.shape
    depth = w_in.shape[0]
    tab = _rope_tables(t)
    wg, wu, wd = (w.astype(MXU_DTYPE) for w in (ffn_w_gate, ffn_w_up, ffn_w_down))
    w_cat, wuq, wukv = _mix_weights(w_in, mla_w_uq, mla_w_ukv)
    cmp_w = _compress_weights(nsa_cmp_pos, nsa_cmp_w1, nsa_cmp_b1, nsa_cmp_w2)
    wo = w_out.astype(MXU_DTYPE)
    sh = lambda a: a.reshape(b, t, a.shape[-1])
    flat = lambda a: a.reshape(b * t, a.shape[-1])
    x2 = x.reshape(b * t, d)
    for l in range(depth):
        lam_init = 0.8 - 0.6 * math.exp(-0.3 * l)
        x2 = _ffn(x2, ffn_norm_g[l, 0], wg, wu, wd, (l, 0))
        (qm, km, vm, qd, kd, vd, qn, kct, vct, ks, vs, kw, vw, gt) = _mixproj(
            x2, t, mix_norm_g[l], w_cat, tab, mla_q_norm_g[l], wuq, mla_kv_norm_g[l], wukv, nsa_gate_b[l], l)
        o_mla = _mla_attention(sh(qm), sh(km), sh(vm))
        o_diff = _diff_attention(sh(qd), sh(kd), sh(vd), diff_lambda[l], diff_norm_g[l], lam_init)
        kc, vc = _compress(sh(kct), sh(vct), cmp_w, l)
        o_nsa = _nsa_attention(sh(qn), kc, vc, sh(ks), sh(vs), sh(kw), sh(vw), sh(gt))
        x2 = _ffn(x2, ffn_norm_g[l, 1], wg, wu, wd, (l, 1),
                  mix=(flat(o_mla), flat(o_diff), flat(o_nsa), wo, l),
                  final_g=final_norm_g if l == depth - 1 else None)
    return x2.reshape(b, t, d)
```

```python
import functools
import math

import numpy as np
import jax
import jax.numpy as jnp
from jax import lax
from jax.experimental import pallas as pl
from jax.experimental.pallas import tpu as pltpu

F32 = jnp.float32
MXU_DTYPE = jnp.bfloat16

LANES = 128
VMEM_LIMIT = 56 * 1024 * 1024

MLA_HEADS, MLA_NOPE, MLA_ROPE, MLA_V = 4, 64, 32, 64
DIFF_HEADS, DIFF_QK, DIFF_V = 4, 32, 64
NSA_HEADS, NSA_GROUPS, NSA_D = 8, 2, 64
CMP_BLOCK, CMP_STRIDE, CMP_HIDDEN = 32, 16, 256
SEL_BLOCK, SEL_TOPK, WINDOW = 64, 16, 512
ROPE_THETA, EPS, NEG, FORCE = 10000.0, 1e-6, -1e30, 1e4
HG = NSA_HEADS // NSA_GROUPS
HEAD_V = 64
assert MLA_V == DIFF_V == NSA_D == HEAD_V

P_CQ, P_CKV, P_DV, P_VCT, P_VS, P_VW, P_NG, P_END = 0, 256, 384, 896, 1024, 1280, 1536, 1792
R_KR, R_DQ, R_DK, R_NQ, R_KCT, R_KS, R_KW, R_END = 0, 128, 384, 640, 1152, 1280, 1536, 1792
T_C32, T_S32, T_C64, T_S64, T_CQ, T_SQ, T_OH, T_END = 0, 128, 256, 384, 512, 640, 768, 896


def _cparams(sem):
    return pltpu.CompilerParams(dimension_semantics=sem, vmem_limit_bytes=VMEM_LIMIT)


def _const_spec(shape):
    n = len(shape)
    return pl.BlockSpec(shape, lambda *_: (0,) * n)


def _layer_spec(stacked, idx):
    n = len(idx)
    zeros = (0,) * (stacked.ndim - n)
    return pl.BlockSpec((None,) * n + stacked.shape[n:], lambda *_: tuple(idx) + zeros)


def _rms(x, g):
    return x * lax.rsqrt(jnp.mean(x * x, axis=-1, keepdims=True) + EPS) * g


def _dot(a, b):
    return jnp.dot(a, b, preferred_element_type=F32)


def _dot_nt(a, b):
    return lax.dot_general(a, b, (((1,), (1,)), ((), ())), preferred_element_type=F32)


def _ffn_body(*refs, f_chunk, mixed, final):
    refs = list(refs)
    x_ref = refs.pop(0)
    x = x_ref[...]
    if mixed:
        o_refs, wo_ref = refs[:3], refs[3]
        refs = refs[4:]
        row = 0
        for o_in in o_refs:
            x = x + _dot(o_in[...], wo_ref[row:row + o_in.shape[1], :])
            row += o_in.shape[1]
    g_ref, wg_ref, wu_ref, wd_ref = refs[:4]
    gf_ref = refs[4] if final else None
    o_ref = refs[-1]
    h = _rms(x, g_ref[...]).astype(MXU_DTYPE)
    d_ff = wg_ref.shape[1]
    acc = jnp.zeros(x.shape, F32)
    for c in range(d_ff // f_chunk):
        sl = slice(c * f_chunk, (c + 1) * f_chunk)
        gate = _dot(h, wg_ref[:, sl])
        up = _dot(h, wu_ref[:, sl])
        act = (gate * jax.nn.sigmoid(gate) * up).astype(MXU_DTYPE)
        acc = acc + _dot(act, wd_ref[sl, :])
    y = x + 0.5 * acc
    if final:
        y = _rms(y, gf_ref[...])
    o_ref[...] = y


def _ffn(x2, g, wg, wu, wd, widx, mix=None, final_g=None, tm=512, f_chunk=256):
    n, d = x2.shape
    d_ff = wg.shape[-1]
    assert n % tm == 0 and d_ff % f_chunk == 0
    final, mixed = final_g is not None, mix is not None
    row = lambda w: pl.BlockSpec((tm, w), lambda i: (i, 0))
    in_specs, args = [row(d)], [x2]
    if mixed:
        *outs, w_out, layer = mix
        in_specs += [row(o.shape[1]) for o in outs] + [_layer_spec(w_out, (layer,))]
        args += list(outs) + [w_out]
    in_specs += [_const_spec((1, d))] + [_layer_spec(w, widx) for w in (wg, wu, wd)]
    args += [g.reshape(1, d), wg, wu, wd]
    if final:
        in_specs.append(_const_spec((1, d)))
        args.append(final_g.reshape(1, d))
    name = "ffn" + ("_mix" if mixed else "") + ("_final" if final else "")
    return pl.pallas_call(
        functools.partial(_ffn_body, f_chunk=f_chunk, mixed=mixed, final=final),
        grid=(n // tm,), in_specs=in_specs, out_specs=row(d),
        out_shape=jax.ShapeDtypeStruct((n, d), F32),
        compiler_params=_cparams(("parallel",)), name=name,
    )(*args)


def _pad_groups(w, width, lead=0):
    g = w.reshape(w.shape[:-1] + (-1, width))
    g = jnp.pad(g, [(0, 0)] * (g.ndim - 1) + [(lead, LANES - width - lead)])
    return g.reshape(w.shape[:-1] + (-1,))


def _mix_weights(w_in, w_uq, w_ukv):
    o = np.cumsum([0, 256, 128, 32, 256, 256, 256, 512, 768, 24]).tolist()
    cq, ckv, kr, dq, dk, dv, nq, nkv, ng = [w_in[..., o[i]:o[i + 1]] for i in range(9)]
    nkv = nkv.reshape(nkv.shape[:-1] + (3, 2, NSA_GROUPS * NSA_D))
    kct, vct = nkv[..., 0, 0, :], nkv[..., 0, 1, :]
    ks, vs = nkv[..., 1, 0, :], nkv[..., 1, 1, :]
    kw, vw = nkv[..., 2, 0, :], nkv[..., 2, 1, :]
    w_cat = jnp.concatenate(
        [cq, ckv, _pad_groups(dv, HEAD_V), vct, _pad_groups(vs, HEAD_V), _pad_groups(vw, HEAD_V),
         _pad_groups(ng, HG * 3),
         _pad_groups(kr, MLA_ROPE, lead=MLA_NOPE), dq, dk, nq, kct, _pad_groups(ks, NSA_D), _pad_groups(kw, NSA_D)],
        axis=-1).astype(MXU_DTYPE)
    assert w_cat.shape[-1] == P_END + R_END
    wuq = _pad_groups(w_uq, MLA_NOPE + MLA_ROPE).astype(MXU_DTYPE)
    ukv = w_ukv.reshape(w_ukv.shape[:-1] + (MLA_HEADS, MLA_NOPE + MLA_V))
    flat = lambda a: a.reshape(a.shape[:-2] + (-1,))
    wukv = jnp.concatenate([_pad_groups(flat(ukv[..., :MLA_NOPE]), MLA_NOPE),
                            _pad_groups(flat(ukv[..., MLA_NOPE:]), MLA_V)], axis=-1).astype(MXU_DTYPE)
    return w_cat, wuq, wukv


def _rope_tables(t):
    def cs(dim):
        inv = ROPE_THETA ** (-jnp.arange(0, dim, 2, dtype=F32) / dim)
        ang = jnp.arange(t, dtype=F32)[:, None] * inv[None, :]
        c, s = jnp.cos(ang), jnp.sin(ang)
        return jnp.concatenate([c, c], axis=1), jnp.concatenate([-s, s], axis=1)

    c32, s32 = cs(32)
    c64, s64 = cs(64)
    one, zero = jnp.ones((t, MLA_NOPE), F32), jnp.zeros((t, MLA_NOPE), F32)
    z32 = jnp.zeros((t, LANES - MLA_NOPE - MLA_ROPE), F32)
    blk = jnp.arange(t)[:, None] // SEL_BLOCK
    onehot = (blk == jnp.arange(LANES)[None, :] - NSA_D).astype(F32)
    return jnp.concatenate([jnp.tile(c32, (1, 4)), jnp.tile(s32, (1, 4)), jnp.tile(c64, (1, 2)),
                            jnp.tile(s64, (1, 2)), jnp.concatenate([one, c32, z32], axis=1),
                            jnp.concatenate([zero, s32, z32], axis=1), onehot], axis=1)


def _mixproj_body(x_ref, g_ref, w_ref, tab_ref, qg_ref, wuq_ref, kvg_ref, wukv_ref, gb_ref,
                  qm_ref, km_ref, vm_ref, qd_ref, kd_ref, vd_ref, qn_ref, kct_ref, vct_ref,
                  ks_ref, vs_ref, kw_ref, vw_ref, gt_ref, chunk_ref):
    h = _rms(x_ref[...], g_ref[...]).astype(MXU_DTYPE)
    yp = _dot(h, w_ref[:, 0:P_END])
    yr = _dot(h, w_ref[:, P_END:P_END + R_END])
    tab = lambda o: tab_ref[:, o:o + LANES]
    lane = lax.broadcasted_iota(jnp.int32, (1, LANES), 1)
    ones_hi = (lane >= HEAD_V).astype(F32)

    def swap_halves(x, dim):
        half = dim // 2
        return jnp.where(lane % dim < half, pltpu.roll(x, LANES - half, 1), pltpu.roll(x, half, 1))

    def rope_of(x, dim, c, s):
        return x * tab(c) + swap_halves(x, dim) * tab(s)

    def rope(off, dim, c, s):
        return rope_of(yr[:, off:off + LANES], dim, c, s)

    def put(ref, j, val):
        ref[:, j * LANES:(j + 1) * LANES] = val.astype(ref.dtype)

    cqn = _rms(yp[:, P_CQ:P_CKV], qg_ref[...]).astype(MXU_DTYPE)
    qa = _dot(cqn, wuq_ref[...])
    q_scale = (MLA_NOPE + MLA_ROPE) ** -0.5
    nq_lanes = MLA_HEADS * LANES
    for hh in range(MLA_HEADS):
        put(qm_ref, hh, rope_of(qa[:, hh * LANES:(hh + 1) * LANES], MLA_ROPE, T_CQ, T_SQ) * q_scale)
    ckvn = _rms(yp[:, P_CKV:P_DV], kvg_ref[...]).astype(MXU_DTYPE)
    kv = _dot(ckvn, wukv_ref[...])
    k_rope = rope(R_KR, MLA_ROPE, T_C32, T_S32)
    for hh in range(MLA_HEADS):
        put(km_ref, hh, kv[:, hh * LANES:(hh + 1) * LANES] + k_rope)
        put(vm_ref, hh, kv[:, nq_lanes:][:, hh * LANES:(hh + 1) * LANES] + ones_hi)
    d_scale = DIFF_QK ** -0.5
    for j in range(2):
        put(qd_ref, j, rope(R_DQ + j * LANES, DIFF_QK, T_C32, T_S32) * d_scale)
        put(kd_ref, j, rope(R_DK + j * LANES, DIFF_QK, T_C32, T_S32))
    for hh in range(DIFF_HEADS):
        put(vd_ref, hh, yp[:, P_DV + hh * LANES:P_DV + (hh + 1) * LANES] + ones_hi)
    n_scale = NSA_D ** -0.5
    for j in range(4):
        put(qn_ref, j, rope(R_NQ + j * LANES, NSA_D, T_C64, T_S64) * n_scale)
    n_chunks = chunk_ref.shape[0] // CMP_STRIDE
    for ref, val in ((kct_ref, rope(R_KCT, NSA_D, T_C64, T_S64)), (vct_ref, yp[:, P_VCT:P_VS])):
        chunk_ref[...] = val
        for tok in range(CMP_STRIDE):
            put(ref, tok, chunk_ref[pl.ds(tok, n_chunks, stride=CMP_STRIDE), :])
    for j in range(NSA_GROUPS):
        put(ks_ref, j, rope(R_KS + j * LANES, NSA_D, T_C64, T_S64) + tab(T_OH))
        put(kw_ref, j, rope(R_KW + j * LANES, NSA_D, T_C64, T_S64))
        for ref, off in ((vs_ref, P_VS), (vw_ref, P_VW)):
            v1 = yp[:, off + j * LANES:off + (j + 1) * LANES] + ones_hi
            put(ref, 2 * j, v1)
            put(ref, 2 * j + 1, pltpu.roll(v1, HEAD_V, 1))
    gt_ref[...] = jax.nn.sigmoid(yp[:, P_NG:P_END] + gb_ref[...])


def _mixproj(x2, t, g, w_cat, tab, qg, wuq, kvg, wukv, gate_b, layer, tm=512):
    n, d = x2.shape
    assert t % tm == 0
    tpb = t // tm
    widths = [512, 512, 512, 256, 256, 512, 512, 128, 128, 256, 512, 256, 512]
    out_shape = [jax.ShapeDtypeStruct((n, w), MXU_DTYPE) for w in widths]
    for i in (7, 8):
        out_shape[i] = jax.ShapeDtypeStruct((n // CMP_STRIDE, CMP_STRIDE * widths[i]), MXU_DTYPE)
    out_shape.append(jax.ShapeDtypeStruct((n, 2 * LANES), F32))
    out_specs = [pl.BlockSpec((tm * s.shape[0] // n, s.shape[1]), lambda i: (i, 0)) for s in out_shape]
    gb = _pad_groups(gate_b.reshape(1, -1), HG * 3)
    in_specs = [pl.BlockSpec((tm, d), lambda i: (i, 0)), _const_spec((1, d)), _layer_spec(w_cat, (layer,)),
                pl.BlockSpec((tm, T_END), lambda i: (i % tpb, 0)),
                _const_spec((1, qg.shape[0])), _layer_spec(wuq, (layer,)),
                _const_spec((1, kvg.shape[0])), _layer_spec(wukv, (layer,)), _const_spec(gb.shape)]
    return pl.pallas_call(
        _mixproj_body, grid=(n // tm,), in_specs=in_specs, out_specs=out_specs, out_shape=out_shape,
        scratch_shapes=[pltpu.VMEM((tm, LANES), F32)],
        compiler_params=_cparams(("parallel",)), name="mixproj",
    )(x2, g.reshape(1, d), w_cat, tab, qg.reshape(1, -1), wuq, kvg.reshape(1, -1), wukv, gb)


def _first_probs(s, m_ref):
    m = jnp.broadcast_to(jnp.max(s, axis=-1, keepdims=True), m_ref.shape)
    m_ref[...] = m
    return jnp.exp(s - jnp.concatenate([m] * (s.shape[1] // LANES), axis=1)).astype(MXU_DTYPE)


def _tile_probs(s, m_ref):
    m_old = m_ref[...]
    m_new = jnp.maximum(m_old, jnp.max(s, axis=-1, keepdims=True))
    m_ref[...] = m_new
    p = jnp.exp(s - jnp.concatenate([m_new] * (s.shape[1] // LANES), axis=1))
    return p.astype(MXU_DTYPE), jnp.exp(m_old - m_new)


def _causal_sweep(n_full, tk, tile_fn):
    def wide(j, carry):
        tile_fn(pl.multiple_of(j * (2 * tk), 2 * tk), 2 * tk)
        return carry

    def single(j, carry):
        tile_fn(pl.multiple_of((n_full - 1) * tk, tk), tk)
        return carry

    lax.fori_loop(0, n_full // 2, wide, 0)
    lax.fori_loop(0, n_full % 2, single, 0)


def _normalised(acc):
    return acc / pltpu.roll(acc, HEAD_V, 1)


def _causal_bias_table(nq, tk):
    off = np.arange(tk // nq)[:, None, None] * nq
    row = np.arange(nq)[None, :, None]
    col = np.arange(tk)[None, None, :]
    return jnp.asarray(np.where(col <= off + row, 0.0, NEG), F32)


def _pair_lanes(o_even, o_odd):
    lane = lax.broadcasted_iota(jnp.int32, o_even.shape, 1)
    return jnp.where(lane < HEAD_V, o_even, pltpu.roll(o_odd, HEAD_V, 1))


def _mla_body(q_ref, k_ref, v_ref, cb_ref, o_ref, m_ref, acc_ref, *, tq, tk, hps):
    q_first = pl.program_id(2) * tq
    n_full = q_first // tk
    qs = [q_ref[0, :, hh * LANES:(hh + 1) * LANES] for hh in range(hps)]

    def tile(k0, nk, first=False):
        for hh in range(hps):
            lanes = slice(hh * LANES, (hh + 1) * LANES)
            s = _dot_nt(qs[hh], k_ref[0, pl.ds(k0, nk), lanes])
            v = v_ref[0, pl.ds(k0, nk), lanes]
            if first:
                p = _first_probs(s + cb_ref[(q_first - n_full * tk) // tq], m_ref.at[hh])
                acc_ref[hh] = _dot(p, v)
            else:
                p, alpha = _tile_probs(s, m_ref.at[hh])
                acc_ref[hh] = alpha * acc_ref[hh] + _dot(p, v)

    tile(pl.multiple_of(n_full * tk, tk), tk, first=True)
    _causal_sweep(n_full, tk, tile)
    for pr in range(hps // 2):
        pair = _pair_lanes(_normalised(acc_ref[2 * pr]), _normalised(acc_ref[2 * pr + 1]))
        o_ref[0, :, pr * LANES:(pr + 1) * LANES] = pair.astype(o_ref.dtype)


def _mla_attention(q, k, v, tq=512, tk=512, hps=4):
    b, t, _ = q.shape
    assert t % tk == 0 and tk % tq == 0 and MLA_HEADS % hps == 0 and hps % 2 == 0
    spec_t = pl.BlockSpec((1, t, hps * LANES), lambda bi, p, i: (bi, 0, p))
    cb = _causal_bias_table(tq, tk)
    return pl.pallas_call(
        functools.partial(_mla_body, tq=tq, tk=tk, hps=hps),
        grid=(b, MLA_HEADS // hps, t // tq),
        in_specs=[pl.BlockSpec((1, tq, hps * LANES), lambda bi, p, i: (bi, i, p)), spec_t, spec_t,
                  _const_spec(cb.shape)],
        out_specs=pl.BlockSpec((1, tq, hps // 2 * LANES), lambda bi, p, i: (bi, i, p)),
        out_shape=jax.ShapeDtypeStruct((b, t, MLA_HEADS // 2 * LANES), MXU_DTYPE),
        scratch_shapes=[pltpu.VMEM((hps, tq, LANES), F32), pltpu.VMEM((hps, tq, LANES), F32)],
        compiler_params=_cparams(("parallel", "parallel", "arbitrary")), name="mla_attn",
    )(q, k, v, cb)


def _diff_body(q_ref, k_ref, v_ref, lam_ref, g_ref, cb_ref, o_ref, m_ref, acc_ref, *, tq, tk, lam_init, npair):
    q_first = pl.program_id(2) * tq
    n_full = q_first // tk
    tile_of = lambda j: slice(j * LANES, (j + 1) * LANES)
    grp = lax.broadcasted_iota(jnp.int32, (tq, LANES), 1) // DIFF_QK
    q4s = []
    for j in range(npair):
        q = q_ref[0, :, tile_of(j)]
        q4s.append(jnp.concatenate([jnp.where(grp == c, q, jnp.zeros_like(q)) for c in range(4)], axis=0))
    head_rows = [slice(0, 2 * tq), slice(2 * tq, 4 * tq)]

    def tile(k0, nk, first=False):
        for j in range(npair):
            s = _dot_nt(q4s[j], k_ref[0, pl.ds(k0, nk), tile_of(j)])
            if first:
                bias = jnp.concatenate([cb_ref[(q_first - n_full * tk) // tq]] * 4, axis=0)
                p = _first_probs(s + bias, m_ref.at[j])
            else:
                p, alpha = _tile_probs(s, m_ref.at[j])
            for hl, r in enumerate(head_rows):
                pv = _dot(p[r], v_ref[0, pl.ds(k0, nk), tile_of(2 * j + hl)])
                acc_ref[j, r, :] = pv if first else alpha[r] * acc_ref[j, r, :] + pv

    tile(pl.multiple_of(n_full * tk, tk), tk, first=True)
    _causal_sweep(n_full, tk, tile)
    lf = lam_ref[...]
    lam = (jnp.exp(jnp.sum(lf[0:1] * lf[1:2], keepdims=True))
           - jnp.exp(jnp.sum(lf[2:3] * lf[3:4], keepdims=True)) + lam_init)
    lane = lax.broadcasted_iota(jnp.int32, (tq, LANES), 1)
    for j in range(npair):
        a = _normalised(acc_ref[j])
        normed = []
        for hl in range(2):
            o = a[2 * hl * tq:(2 * hl + 1) * tq] - lam * a[(2 * hl + 1) * tq:(2 * hl + 2) * tq]
            ms = jnp.sum(jnp.where(lane < DIFF_V, o * o, 0.0), axis=-1, keepdims=True) * (1.0 / DIFF_V)
            normed.append(o * lax.rsqrt(ms + EPS))
        out = _pair_lanes(normed[0], normed[1]) * g_ref[...] * (1.0 - lam_init)
        o_ref[0, :, tile_of(j)] = out.astype(o_ref.dtype)


def _diff_attention(q, k, v, diff_lambda, norm_g, lam_init, tq=512, tk=512, npair=2):
    b, t, _ = q.shape
    pairs = DIFF_HEADS // 2
    assert t % tk == 0 and tk % tq == 0 and pairs % npair == 0
    lam_pad = jnp.pad(diff_lambda.astype(F32), ((0, 4), (0, LANES - DIFF_QK)))
    g2 = jnp.tile(norm_g.astype(F32), 2).reshape(1, LANES)
    spec_qo = pl.BlockSpec((1, tq, npair * LANES), lambda bi, p, i: (bi, i, p))
    cb = _causal_bias_table(tq, tk)
    return pl.pallas_call(
        functools.partial(_diff_body, tq=tq, tk=tk, lam_init=lam_init, npair=npair),
        grid=(b, pairs // npair, t // tq),
        in_specs=[spec_qo, pl.BlockSpec((1, t, npair * LANES), lambda bi, p, i: (bi, 0, p)),
                  pl.BlockSpec((1, t, npair * 2 * LANES), lambda bi, p, i: (bi, 0, p)),
                  _const_spec(lam_pad.shape), _const_spec(g2.shape), _const_spec(cb.shape)],
        out_specs=spec_qo,
        out_shape=jax.ShapeDtypeStruct((b, t, pairs * LANES), MXU_DTYPE),
        scratch_shapes=[pltpu.VMEM((npair, 4 * tq, LANES), F32), pltpu.VMEM((npair, 4 * tq, LANES), F32)],
        compiler_params=_cparams(("parallel", "parallel", "arbitrary")), name="diff_attn",
    )(q, k, v, lam_pad, g2, cb)


def _compress_body(xk_ref, xv_ref, w1_ref, w1o_ref, pos_ref, b1_ref, w2_ref, kc_ref, vc_ref):
    for kv, (x_ref, o_ref) in enumerate(((xk_ref, kc_ref), (xv_ref, vc_ref))):
        x = x_ref[0]
        posb = _dot(jnp.broadcast_to(pos_ref[kv], (8, pos_ref.shape[-1])), w1o_ref[kv])[0:1] + b1_ref[kv]
        for g in range(NSA_GROUPS):
            ab = _dot(x, w1_ref[kv, g])
            first, second = ab[:, :CMP_HIDDEN], ab[:, CMP_HIDDEN:]
            nxt = jnp.concatenate([second[1:], second[:1]], axis=0)
            hid = first + nxt + posb
            act = (hid * jax.nn.sigmoid(hid)).astype(MXU_DTYPE)
            out = _dot(act, w2_ref[kv]).astype(o_ref.dtype)
            if kv == 0:
                o_ref[0, :, g * LANES:(g + 1) * LANES] = out[:, :LANES]
            else:
                o_ref[0, :, 2 * g * LANES:2 * (g + 1) * LANES] = out


def _compress_weights(pos, w1, b1, w2):
    z = w1.shape[0]
    half = CMP_BLOCK // 2
    assert half == CMP_STRIDE
    w1r = w1.reshape(z, 2, 2, half, NSA_D, CMP_HIDDEN)
    both = jnp.concatenate([w1r[:, :, 0], w1r[:, :, 1]], axis=-1).astype(MXU_DTYPE)
    eye = jnp.eye(NSA_GROUPS, dtype=MXU_DTYPE)
    w1g = both[:, :, None, :, None, :, :] * eye[None, None, :, None, :, None, None]
    w1g = w1g.reshape(z, 2, NSA_GROUPS, half * NSA_GROUPS * NSA_D, 2 * CMP_HIDDEN)
    zeros = jnp.zeros(w2.shape[:-1] + (2 * (LANES - NSA_D),), w2.dtype)
    w2p = jnp.concatenate([w2, zeros, w2], axis=-1).astype(MXU_DTYPE)
    posf = pos.reshape(z, 2, 1, CMP_BLOCK * NSA_D).astype(MXU_DTYPE)
    return w1g, w1.astype(MXU_DTYPE), posf, b1.reshape(z, 2, 1, CMP_HIDDEN).astype(F32), w2p


def _compress(kct, vct, weights, layer):
    b, nch, _ = kct.shape
    xk, xv = kct, vct
    args = [xk, xv] + list(weights)
    x_spec = pl.BlockSpec((1, nch, xk.shape[-1]), lambda bi: (bi, 0, 0))
    o_spec = lambda w: pl.BlockSpec((1, nch, w), lambda bi: (bi, 0, 0))
    widths = (NSA_GROUPS * LANES, 2 * NSA_GROUPS * LANES)
    return pl.pallas_call(
        _compress_body, grid=(b,),
        in_specs=[x_spec, x_spec] + [_layer_spec(a, (layer,)) for a in args[2:]],
        out_specs=[o_spec(w) for w in widths],
        out_shape=[jax.ShapeDtypeStruct((b, nch, w), MXU_DTYPE) for w in widths],
        compiler_params=_cparams(("parallel",)), name="nsa_compress",
    )(*args)


def _sel_map_t(nch):
    r_sel, r_cmp = SEL_BLOCK // CMP_STRIDE, CMP_BLOCK // CMP_STRIDE
    nc = nch - 1
    mt = np.zeros((nch * CMP_STRIDE // SEL_BLOCK, nch), np.float32)
    for j in range(mt.shape[0]):
        for m in range(r_sel):
            for n in range(r_cmp):
                idx = j * r_sel - m - n
                if 0 <= idx < nc:
                    mt[j, idx] += 1.0
    return mt


def _split3(x):
    hi = x.astype(MXU_DTYPE)
    r1 = x - hi.astype(F32)
    mid = r1.astype(MXU_DTYPE)
    lo = (r1 - mid.astype(F32)).astype(MXU_DTYPE)
    return hi, mid, lo


def _nsa_body(q_ref, kc_ref, vc_ref, ks_ref, vs_ref, kw_ref, vw_ref, gt_ref, mt_ref, ge_ref, cb_ref, wb_ref,
              o_ref, m_ref, acc_ref, *, nq, tk, win_rows, ng):
    rows = HG * nq
    q_first = pl.program_id(2) * nq
    blk_first = q_first // SEL_BLOCK
    nb = mt_ref.shape[0]
    lane = lax.broadcasted_iota(jnp.int32, (rows, LANES), 1)
    qpos = q_first + lax.broadcasted_iota(jnp.int32, (nq, 1), 0)
    tpos = jnp.concatenate([qpos] * HG, axis=0)
    is_q = lane < NSA_D
    even, odd = slice(0, 2 * nq), slice(2 * nq, 4 * nq)
    jrow = lax.broadcasted_iota(jnp.int32, (nb, nq), 0)
    cblk = blk_first + lax.broadcasted_iota(jnp.int32, (nb, nq), 1) // SEL_BLOCK
    tiles = lambda g, w, j=0: slice((g * w + j) * LANES, (g * w + j + 1) * LANES)

    def pv(p, v_ref, g, r0, nr):
        return jnp.concatenate([_dot(p[even], v_ref[0, pl.ds(r0, nr), tiles(g, 2, 0)]),
                                _dot(p[odd], v_ref[0, pl.ds(r0, nr), tiles(g, 2, 1)])], axis=0)

    def before_rank(g):
        qe = q_ref[0, :, tiles(g, 2, 0)].astype(F32)
        qo = q_ref[0, :, tiles(g, 2, 1)].astype(F32)
        q4 = jnp.concatenate([qe, qo, pltpu.roll(qe, NSA_D, 1), pltpu.roll(qo, NSA_D, 1)], axis=0)
        q_plain = jnp.where(is_q, q4, 0.0).astype(MXU_DTYPE)

        w0 = pl.multiple_of(jnp.maximum(q_first - WINDOW, 0), nq)
        wbias = wb_ref[jnp.minimum(q_first // nq, WINDOW // nq)]
        s = _dot_nt(q_plain, kw_ref[0, pl.ds(w0, win_rows), tiles(g, 1)]) + jnp.concatenate([wbias] * HG, axis=0)
        e = jnp.exp(s - jnp.max(s, axis=-1, keepdims=True))
        acc_win = pv(e.astype(MXU_DTYPE), vw_ref, g, w0, win_rows)
        g_all = sum(_dot(part, ge_ref[...]) for part in _split3(gt_ref[0, :, tiles(g, 1)]))

        s = _dot_nt(q_plain, kc_ref[0, :, tiles(g, 1)])
        n_idx = lax.broadcasted_iota(jnp.int32, s.shape, 1)
        valid = n_idx * CMP_STRIDE + (CMP_BLOCK - 1) <= tpos
        sm = jnp.where(valid, s, NEG)
        e = jnp.where(valid, jnp.exp(sm - jnp.max(sm, axis=-1, keepdims=True)), 0.0)
        den = jnp.sum(e, axis=-1, keepdims=True)
        p = e * jnp.where(den > 0.0, 1.0 / den, 0.0)
        o_cmp = pv(p.astype(MXU_DTYPE), vc_ref, g, 0, vc_ref.shape[1])

        pg = p[0:nq]
        for hl in range(1, HG):
            pg = pg + p[hl * nq:(hl + 1) * nq]
        p_slc_t = sum(_dot_nt(mt_ref[...], part) for part in _split3(pg))
        forced = (jrow == 0) | (jrow == cblk) | (jrow == cblk - 1)
        score = jnp.where(jrow > cblk, -1.0, p_slc_t + jnp.where(forced, FORCE, 0.0))
        return dict(q4=q4, acc_win=acc_win, g_all=g_all, o_cmp=o_cmp, score=score)

    st = [before_rank(g) for g in range(ng)]
    sub, piece = 8, 16
    last_blk = blk_first + nq // SEL_BLOCK - 1
    jl = lax.broadcasted_iota(jnp.int32, (sub, nq), 0)

    n_cnt = piece // sub

    def count_piece(cnts, ib, jb):
        cnts = list(cnts)
        for g in range(ng):
            score = st[g]["score"]
            for i in range(ib * piece, (ib + 1) * piece):
                row = jnp.broadcast_to(score[i:i + 1], (sub, nq))
                for h in range(n_cnt):
                    j0 = jb * piece + h * sub
                    grp = score[j0:j0 + sub]
                    if j0 > i:
                        ahead = row >= grp
                    elif j0 + sub - 1 < i:
                        ahead = row > grp
                    else:
                        ahead = (row > grp) | ((row == grp) & (jl > i - j0))
                    cnts[g * n_cnt + h] = cnts[g * n_cnt + h] + jnp.where(ahead, 1.0, 0.0)
        return tuple(cnts)

    cnt = [[] for _ in range(ng)]
    for jb in range(nb // piece):
        cnts = tuple(jnp.zeros((sub, nq), F32) for _ in range(ng * n_cnt))
        for ib in range(nb // piece):
            needed = max(ib, jb) * piece <= last_blk
            cnts = lax.cond(needed, functools.partial(count_piece, ib=ib, jb=jb), lambda c: c, cnts)
        for g in range(ng):
            cnt[g].extend(cnts[g * n_cnt:(g + 1) * n_cnt])

    n_full = q_first // tk
    k_diag = pl.multiple_of(n_full * tk, tk)

    def after_rank(g):
        sel_t = (jnp.concatenate(cnt[g], axis=0) < float(SEL_TOPK)) & (jrow <= cblk)
        neg_t = jnp.where(sel_t, 0.0, NEG)
        bias = []
        for c in range(nq // LANES):
            pieces = [jnp.zeros((NSA_D, LANES), F32), neg_t[:, c * LANES:(c + 1) * LANES]]
            if LANES - NSA_D - nb:
                pieces.append(jnp.zeros((LANES - NSA_D - nb, LANES), F32))
            bias.append(jnp.concatenate(pieces, axis=0).T)
        bias = jnp.concatenate(bias, axis=0)
        q_aug = jnp.where(is_q, st[g]["q4"], jnp.concatenate([bias] * HG, axis=0)).astype(MXU_DTYPE)
        s = _dot_nt(q_aug, ks_ref[0, pl.ds(k_diag, tk), tiles(g, 1)])
        p = _first_probs(s + jnp.concatenate([cb_ref[(q_first - n_full * tk) // nq]] * HG, axis=0), m_ref.at[g])
        acc_ref[g] = pv(p, vs_ref, g, k_diag, tk)
        return q_aug

    q_augs = [after_rank(g) for g in range(ng)]

    def slc_tile(k0, nk):
        for g in range(ng):
            p, alpha = _tile_probs(_dot_nt(q_augs[g], ks_ref[0, pl.ds(k0, nk), tiles(g, 1)]), m_ref.at[g])
            acc_ref[g] = alpha * acc_ref[g] + pv(p, vs_ref, g, k0, nk)

    _causal_sweep(n_full, tk, slc_tile)

    first = lax.broadcasted_iota(jnp.int32, (nq, LANES), 1) < HEAD_V
    for g in range(ng):
        g_all, o_cmp = st[g]["g_all"], st[g]["o_cmp"]
        for pr in range(HG // 2):
            ev, od = slice(pr * nq, (pr + 1) * nq), slice((2 + pr) * nq, (3 + pr) * nq)
            gates = [g_all[:, (3 * pr + br) * LANES:(3 * pr + br + 1) * LANES] for br in range(3)]
            out = gates[0] * jnp.where(first, o_cmp[ev], o_cmp[od])
            for gate, acc in zip(gates[1:], (acc_ref[g], st[g]["acc_win"])):
                num = jnp.where(first, acc[ev], acc[od])
                den = pltpu.roll(jnp.where(first, acc[od], acc[ev]), HEAD_V, 1)
                out = out + gate * (num / den)
            o_ref[0, :, tiles(g, 2, pr)] = out.astype(o_ref.dtype)


def _gate_spread():
    e = np.zeros((LANES, (HG // 2) * 3 * LANES), np.float32)
    for hl in range(HG):
        for br in range(3):
            c0 = ((hl // 2) * 3 + br) * LANES + (hl % 2) * HEAD_V
            e[hl * 3 + br, c0:c0 + HEAD_V] = 1.0
    return e


def _window_bias_table(nq, win_rows):
    tabs = []
    for o in range(WINDOW // nq + 1):
        delta = (o * nq + np.arange(nq)[:, None]) - (max(o * nq - WINDOW, 0) + np.arange(win_rows)[None, :])
        tabs.append(np.where((delta >= 0) & (delta < WINDOW), 0.0, NEG))
    return jnp.asarray(np.stack(tabs), F32)


def _nsa_attention(q, kc, vc, ks, vs, kw, vw, gates, nq=256, tk=512, ng=2):
    b, t, _ = q.shape
    nch = kc.shape[1]
    win_rows = WINDOW + nq
    assert t % tk == 0 and tk % nq == 0 and nq % LANES == 0 and WINDOW % nq == 0 and t >= win_rows and HG == 4
    assert NSA_GROUPS % ng == 0
    mt = jnp.asarray(_sel_map_t(nch), MXU_DTYPE)
    nb = mt.shape[0]
    assert nb <= LANES - NSA_D
    consts = [mt, jnp.asarray(_gate_spread(), MXU_DTYPE), _causal_bias_table(nq, tk), _window_bias_table(nq, win_rows)]
    rows = HG * nq
    spec_q = pl.BlockSpec((1, nq, ng * 2 * LANES), lambda bi, g, i: (bi, i, g))
    spec_kc = pl.BlockSpec((1, nch, ng * LANES), lambda bi, g, i: (bi, 0, g))
    spec_vc = pl.BlockSpec((1, nch, ng * 2 * LANES), lambda bi, g, i: (bi, 0, g))
    spec_k = pl.BlockSpec((1, t, ng * LANES), lambda bi, g, i: (bi, 0, g))
    spec_v = pl.BlockSpec((1, t, ng * 2 * LANES), lambda bi, g, i: (bi, 0, g))
    spec_g = pl.BlockSpec((1, nq, ng * LANES), lambda bi, g, i: (bi, i, g))
    return pl.pallas_call(
        functools.partial(_nsa_body, nq=nq, tk=tk, win_rows=win_rows, ng=ng),
        grid=(b, NSA_GROUPS // ng, t // nq),
        in_specs=[spec_q, spec_kc, spec_vc, spec_k, spec_v, spec_k, spec_v, spec_g] + [_const_spec(c.shape) for c in consts],
        out_specs=spec_q,
        out_shape=jax.ShapeDtypeStruct((b, t, NSA_GROUPS * 2 * LANES), MXU_DTYPE),
        scratch_shapes=[pltpu.VMEM((ng, rows, LANES), F32), pltpu.VMEM((ng, rows, LANES), F32)],
        compiler_params=_cparams(("parallel", "parallel", "arbitrary")), name="nsa_attn",
    )(q, kc, vc, ks, vs, kw, vw, gates, *consts)


def kernel(x, ffn_norm_g, ffn_w_gate, ffn_w_up, ffn_w_down, mix_norm_g, w_in, mla_q_norm_g, mla_w_uq, mla_kv_norm_g, mla_w_ukv, diff_lambda, diff_norm_g, nsa_cmp_pos, nsa_cmp_w1, nsa_cmp_b1, nsa_cmp_w2, nsa_gate_b, w_out, final_norm_g):
    b, t, d = x.shape
    depth = w_in.shape[0]
    tab = _rope_tables(t)
    wg, wu, wd = (w.astype(MXU_DTYPE) for w in (ffn_w_gate, ffn_w_up, ffn_w_down))
    w_cat, wuq, wukv = _mix_weights(w_in, mla_w_uq, mla_w_ukv)
    cmp_w = _compress_weights(nsa_cmp_pos, nsa_cmp_w1, nsa_cmp_b1, nsa_cmp_w2)
    wo = w_out.astype(MXU_DTYPE)
    sh = lambda a: a.reshape(b, t, a.shape[-1])
    flat = lambda a: a.reshape(b * t, a.shape[-1])
    x2 = x.reshape(b * t, d)
    for l in range(depth):
        lam_init = 0.8 - 0.6 * math.exp(-0.3 * l)
        x2 = _ffn(x2, ffn_norm_g[l, 0], wg, wu, wd, (l, 0))
        (qm, km, vm, qd, kd, vd, qn, kct, vct, ks, vs, kw, vw, gt) = _mixproj(
            x2, t, mix_norm_g[l], w_cat, tab, mla_q_norm_g[l], wuq, mla_kv_norm_g[l], wukv, nsa_gate_b[l], l)
        o_mla = _mla_attention(sh(qm), sh(km), sh(vm))
        o_diff = _diff_attention(sh(qd), sh(kd), sh(vd), diff_lambda[l], diff_norm_g[l], lam_init)
        chunks = lambda a: a.reshape(b, t // CMP_STRIDE, a.shape[-1])
        kc, vc = _compress(chunks(kct), chunks(vct), cmp_w, l)
        o_nsa = _nsa_attention(sh(qn), kc, vc, sh(ks), sh(vs), sh(kw), sh(vw), sh(gt))
        x2 = _ffn(x2, ffn_norm_g[l, 1], wg, wu, wd, (l, 1),
                  mix=(flat(o_mla), flat(o_diff), flat(o_nsa), wo, l),
                  final_g=final_norm_g if l == depth - 1 else None)
    return x2.reshape(b, t, d)
```

```python
import functools
import math

import numpy as np
import jax
import jax.numpy as jnp
from jax import lax
from jax.experimental import pallas as pl
from jax.experimental.pallas import tpu as pltpu

F32 = jnp.float32
MXU_DTYPE = jnp.bfloat16

LANES = 128
VMEM_LIMIT = 56 * 1024 * 1024

MLA_HEADS, MLA_NOPE, MLA_ROPE, MLA_V = 4, 64, 32, 64
DIFF_HEADS, DIFF_QK, DIFF_V = 4, 32, 64
NSA_HEADS, NSA_GROUPS, NSA_D = 8, 2, 64
CMP_BLOCK, CMP_STRIDE, CMP_HIDDEN = 32, 16, 256
SEL_BLOCK, SEL_TOPK, WINDOW = 64, 16, 512
ROPE_THETA, EPS, NEG, FORCE = 10000.0, 1e-6, -1e30, 1e4
HG = NSA_HEADS // NSA_GROUPS
HEAD_V = 64
assert MLA_V == DIFF_V == NSA_D == HEAD_V

P_CQ, P_CKV, P_DV, P_VCT, P_VS, P_VW, P_NG, P_END = 0, 256, 384, 896, 1024, 1280, 1536, 1792
R_KR, R_DQ, R_DK, R_NQ, R_KCT, R_KS, R_KW, R_END = 0, 128, 384, 640, 1152, 1280, 1536, 1792
T_C32, T_S32, T_C64, T_S64, T_CQ, T_SQ, T_OH, T_END = 0, 128, 256, 384, 512, 640, 768, 896


def _cparams(sem):
    return pltpu.CompilerParams(dimension_semantics=sem, vmem_limit_bytes=VMEM_LIMIT)


def _const_spec(shape):
    n = len(shape)
    return pl.BlockSpec(shape, lambda *_: (0,) * n)


def _layer_spec(stacked, idx):
    n = len(idx)
    zeros = (0,) * (stacked.ndim - n)
    return pl.BlockSpec((None,) * n + stacked.shape[n:], lambda *_: tuple(idx) + zeros)


def _rms(x, g):
    return x * lax.rsqrt(jnp.mean(x * x, axis=-1, keepdims=True) + EPS) * g


def _dot(a, b):
    return jnp.dot(a, b, preferred_element_type=F32)


def _dot_nt(a, b):
    return lax.dot_general(a, b, (((1,), (1,)), ((), ())), preferred_element_type=F32)


def _ffn_body(*refs, f_chunk, mixed, final):
    refs = list(refs)
    x_ref = refs.pop(0)
    x = x_ref[...]
    if mixed:
        o_refs, wo_ref = refs[:3], refs[3]
        refs = refs[4:]
        row = 0
        for o_in in o_refs:
            x = x + _dot(o_in[...], wo_ref[row:row + o_in.shape[1], :])
            row += o_in.shape[1]
    g_ref, wg_ref, wu_ref, wd_ref = refs[:4]
    gf_ref = refs[4] if final else None
    o_ref = refs[-1]
    h = _rms(x, g_ref[...]).astype(MXU_DTYPE)
    d_ff = wg_ref.shape[1]
    acc = jnp.zeros(x.shape, F32)
    for c in range(d_ff // f_chunk):
        sl = slice(c * f_chunk, (c + 1) * f_chunk)
        gate = _dot(h, wg_ref[:, sl])
        up = _dot(h, wu_ref[:, sl])
        act = (gate * jax.nn.sigmoid(gate) * up).astype(MXU_DTYPE)
        acc = acc + _dot(act, wd_ref[sl, :])
    y = x + 0.5 * acc
    if final:
        y = _rms(y, gf_ref[...])
    o_ref[...] = y


def _ffn(x2, g, wg, wu, wd, widx, mix=None, final_g=None, tm=512, f_chunk=256):
    n, d = x2.shape
    d_ff = wg.shape[-1]
    assert n % tm == 0 and d_ff % f_chunk == 0
    final, mixed = final_g is not None, mix is not None
    row = lambda w: pl.BlockSpec((tm, w), lambda i: (i, 0))
    in_specs, args = [row(d)], [x2]
    if mixed:
        *outs, w_out, layer = mix
        in_specs += [row(o.shape[1]) for o in outs] + [_layer_spec(w_out, (layer,))]
        args += list(outs) + [w_out]
    in_specs += [_const_spec((1, d))] + [_layer_spec(w, widx) for w in (wg, wu, wd)]
    args += [g.reshape(1, d), wg, wu, wd]
    if final:
        in_specs.append(_const_spec((1, d)))
        args.append(final_g.reshape(1, d))
    name = "ffn" + ("_mix" if mixed else "") + ("_final" if final else "")
    return pl.pallas_call(
        functools.partial(_ffn_body, f_chunk=f_chunk, mixed=mixed, final=final),
        grid=(n // tm,), in_specs=in_specs, out_specs=row(d),
        out_shape=jax.ShapeDtypeStruct((n, d), F32),
        compiler_params=_cparams(("parallel",)), name=name,
    )(*args)


def _pad_groups(w, width, lead=0):
    g = w.reshape(w.shape[:-1] + (-1, width))
    g = jnp.pad(g, [(0, 0)] * (g.ndim - 1) + [(lead, LANES - width - lead)])
    return g.reshape(w.shape[:-1] + (-1,))


def _mix_weights(w_in, w_uq, w_ukv):
    o = np.cumsum([0, 256, 128, 32, 256, 256, 256, 512, 768, 24]).tolist()
    cq, ckv, kr, dq, dk, dv, nq, nkv, ng = [w_in[..., o[i]:o[i + 1]] for i in range(9)]
    nkv = nkv.reshape(nkv.shape[:-1] + (3, 2, NSA_GROUPS * NSA_D))
    kct, vct = nkv[..., 0, 0, :], nkv[..., 0, 1, :]
    ks, vs = nkv[..., 1, 0, :], nkv[..., 1, 1, :]
    kw, vw = nkv[..., 2, 0, :], nkv[..., 2, 1, :]
    w_cat = jnp.concatenate(
        [cq, ckv, _pad_groups(dv, HEAD_V), vct, _pad_groups(vs, HEAD_V), _pad_groups(vw, HEAD_V),
         _pad_groups(ng, HG * 3),
         _pad_groups(kr, MLA_ROPE, lead=MLA_NOPE), dq, dk, nq, kct, _pad_groups(ks, NSA_D), _pad_groups(kw, NSA_D)],
        axis=-1).astype(MXU_DTYPE)
    assert w_cat.shape[-1] == P_END + R_END
    wuq = _pad_groups(w_uq, MLA_NOPE + MLA_ROPE).astype(MXU_DTYPE)
    ukv = w_ukv.reshape(w_ukv.shape[:-1] + (MLA_HEADS, MLA_NOPE + MLA_V))
    flat = lambda a: a.reshape(a.shape[:-2] + (-1,))
    wukv = jnp.concatenate([_pad_groups(flat(ukv[..., :MLA_NOPE]), MLA_NOPE),
                            _pad_groups(flat(ukv[..., MLA_NOPE:]), MLA_V)], axis=-1).astype(MXU_DTYPE)
    return w_cat, wuq, wukv


def _rope_tables(t):
    def cs(dim):
        inv = ROPE_THETA ** (-jnp.arange(0, dim, 2, dtype=F32) / dim)
        ang = jnp.arange(t, dtype=F32)[:, None] * inv[None, :]
        c, s = jnp.cos(ang), jnp.sin(ang)
        return jnp.concatenate([c, c], axis=1), jnp.concatenate([-s, s], axis=1)

    c32, s32 = cs(32)
    c64, s64 = cs(64)
    one, zero = jnp.ones((t, MLA_NOPE), F32), jnp.zeros((t, MLA_NOPE), F32)
    z32 = jnp.zeros((t, LANES - MLA_NOPE - MLA_ROPE), F32)
    blk = jnp.arange(t)[:, None] // SEL_BLOCK
    onehot = (blk == jnp.arange(LANES)[None, :] - NSA_D).astype(F32)
    return jnp.concatenate([jnp.tile(c32, (1, 4)), jnp.tile(s32, (1, 4)), jnp.tile(c64, (1, 2)),
                            jnp.tile(s64, (1, 2)), jnp.concatenate([one, c32, z32], axis=1),
                            jnp.concatenate([zero, s32, z32], axis=1), onehot], axis=1)


def _mixproj_body(x_ref, g_ref, w_ref, tab_ref, qg_ref, wuq_ref, kvg_ref, wukv_ref, gb_ref,
                  qm_ref, km_ref, vm_ref, qd_ref, kd_ref, vd_ref, qn_ref, kct_ref, vct_ref,
                  ks_ref, vs_ref, kw_ref, vw_ref, gt_ref, chunk_ref):
    h = _rms(x_ref[...], g_ref[...]).astype(MXU_DTYPE)
    yp = _dot(h, w_ref[:, 0:P_END])
    yr = _dot(h, w_ref[:, P_END:P_END + R_END])
    tab = lambda o: tab_ref[:, o:o + LANES]
    lane = lax.broadcasted_iota(jnp.int32, (1, LANES), 1)
    ones_hi = (lane >= HEAD_V).astype(F32)

    def swap_halves(x, dim):
        half = dim // 2
        return jnp.where(lane % dim < half, pltpu.roll(x, LANES - half, 1), pltpu.roll(x, half, 1))

    def rope_of(x, dim, c, s):
        return x * tab(c) + swap_halves(x, dim) * tab(s)

    def rope(off, dim, c, s):
        return rope_of(yr[:, off:off + LANES], dim, c, s)

    def put(ref, j, val):
        ref[:, j * LANES:(j + 1) * LANES] = val.astype(ref.dtype)

    cqn = _rms(yp[:, P_CQ:P_CKV], qg_ref[...]).astype(MXU_DTYPE)
    qa = _dot(cqn, wuq_ref[...])
    q_scale = (MLA_NOPE + MLA_ROPE) ** -0.5
    nq_lanes = MLA_HEADS * LANES
    for hh in range(MLA_HEADS):
        put(qm_ref, hh, rope_of(qa[:, hh * LANES:(hh + 1) * LANES], MLA_ROPE, T_CQ, T_SQ) * q_scale)
    ckvn = _rms(yp[:, P_CKV:P_DV], kvg_ref[...]).astype(MXU_DTYPE)
    kv = _dot(ckvn, wukv_ref[...])
    k_rope = rope(R_KR, MLA_ROPE, T_C32, T_S32)
    for hh in range(MLA_HEADS):
        put(km_ref, hh, kv[:, hh * LANES:(hh + 1) * LANES] + k_rope)
        put(vm_ref, hh, kv[:, nq_lanes:][:, hh * LANES:(hh + 1) * LANES] + ones_hi)
    d_scale = DIFF_QK ** -0.5
    for j in range(2):
        put(qd_ref, j, rope(R_DQ + j * LANES, DIFF_QK, T_C32, T_S32) * d_scale)
        put(kd_ref, j, rope(R_DK + j * LANES, DIFF_QK, T_C32, T_S32))
    for hh in range(DIFF_HEADS):
        put(vd_ref, hh, yp[:, P_DV + hh * LANES:P_DV + (hh + 1) * LANES] + ones_hi)
    n_scale = NSA_D ** -0.5
    for j in range(4):
        put(qn_ref, j, rope(R_NQ + j * LANES, NSA_D, T_C64, T_S64) * n_scale)
    n_chunks = chunk_ref.shape[0] // CMP_STRIDE
    for ref, val in ((kct_ref, rope(R_KCT, NSA_D, T_C64, T_S64)), (vct_ref, yp[:, P_VCT:P_VS])):
        chunk_ref[...] = val
        for tok in range(CMP_STRIDE):
            put(ref, tok, chunk_ref[pl.ds(tok, n_chunks, stride=CMP_STRIDE), :])
    for j in range(NSA_GROUPS):
        put(ks_ref, j, rope(R_KS + j * LANES, NSA_D, T_C64, T_S64) + tab(T_OH))
        put(kw_ref, j, rope(R_KW + j * LANES, NSA_D, T_C64, T_S64))
        for ref, off in ((vs_ref, P_VS), (vw_ref, P_VW)):
            v1 = yp[:, off + j * LANES:off + (j + 1) * LANES] + ones_hi
            put(ref, 2 * j, v1)
            put(ref, 2 * j + 1, pltpu.roll(v1, HEAD_V, 1))
    gt_ref[...] = jax.nn.sigmoid(yp[:, P_NG:P_END] + gb_ref[...])


def _mixproj(x2, t, g, w_cat, tab, qg, wuq, kvg, wukv, gate_b, layer, tm=512):
    n, d = x2.shape
    assert t % tm == 0
    tpb = t // tm
    widths = [512, 512, 512, 256, 256, 512, 512, 128, 128, 256, 512, 256, 512]
    out_shape = [jax.ShapeDtypeStruct((n, w), MXU_DTYPE) for w in widths]
    for i in (7, 8):
        out_shape[i] = jax.ShapeDtypeStruct((n // CMP_STRIDE, CMP_STRIDE * widths[i]), MXU_DTYPE)
    out_shape.append(jax.ShapeDtypeStruct((n, 2 * LANES), F32))
    out_specs = [pl.BlockSpec((tm * s.shape[0] // n, s.shape[1]), lambda i: (i, 0)) for s in out_shape]
    gb = _pad_groups(gate_b.reshape(1, -1), HG * 3)
    in_specs = [pl.BlockSpec((tm, d), lambda i: (i, 0)), _const_spec((1, d)), _layer_spec(w_cat, (layer,)),
                pl.BlockSpec((tm, T_END), lambda i: (i % tpb, 0)),
                _const_spec((1, qg.shape[0])), _layer_spec(wuq, (layer,)),
                _const_spec((1, kvg.shape[0])), _layer_spec(wukv, (layer,)), _const_spec(gb.shape)]
    return pl.pallas_call(
        _mixproj_body, grid=(n // tm,), in_specs=in_specs, out_specs=out_specs, out_shape=out_shape,
        scratch_shapes=[pltpu.VMEM((tm, LANES), F32)],
        compiler_params=_cparams(("parallel",)), name="mixproj",
    )(x2, g.reshape(1, d), w_cat, tab, qg.reshape(1, -1), wuq, kvg.reshape(1, -1), wukv, gb)


def _first_probs(s, m_ref):
    m = jnp.broadcast_to(jnp.max(s, axis=-1, keepdims=True), m_ref.shape)
    m_ref[...] = m
    return jnp.exp(s - jnp.concatenate([m] * (s.shape[1] // LANES), axis=1)).astype(MXU_DTYPE)


def _tile_probs(s, m_ref):
    m_old = m_ref[...]
    m_new = jnp.maximum(m_old, jnp.max(s, axis=-1, keepdims=True))
    m_ref[...] = m_new
    p = jnp.exp(s - jnp.concatenate([m_new] * (s.shape[1] // LANES), axis=1))
    return p.astype(MXU_DTYPE), jnp.exp(m_old - m_new)


def _causal_sweep(n_full, tk, tile_fn):
    def wide(j, carry):
        tile_fn(pl.multiple_of(j * (2 * tk), 2 * tk), 2 * tk)
        return carry

    def single(j, carry):
        tile_fn(pl.multiple_of((n_full - 1) * tk, tk), tk)
        return carry

    lax.fori_loop(0, n_full // 2, wide, 0)
    lax.fori_loop(0, n_full % 2, single, 0)


def _normalised(acc):
    return acc / pltpu.roll(acc, HEAD_V, 1)


def _causal_bias_table(nq, tk):
    off = np.arange(tk // nq)[:, None, None] * nq
    row = np.arange(nq)[None, :, None]
    col = np.arange(tk)[None, None, :]
    return jnp.asarray(np.where(col <= off + row, 0.0, NEG), F32)


def _pair_lanes(o_even, o_odd):
    lane = lax.broadcasted_iota(jnp.int32, o_even.shape, 1)
    return jnp.where(lane < HEAD_V, o_even, pltpu.roll(o_odd, HEAD_V, 1))


def _mla_body(q_ref, k_ref, v_ref, cb_ref, o_ref, m_ref, acc_ref, *, tq, tk, hps, bps):
    q_first = pl.program_id(2) * tq
    n_full = q_first // tk
    chains = [(bb, hh) for bb in range(bps) for hh in range(hps)]
    qs = [q_ref[bb, :, hh * LANES:(hh + 1) * LANES] for bb, hh in chains]

    def tile(k0, nk, first=False):
        for c, (bb, hh) in enumerate(chains):
            lanes = slice(hh * LANES, (hh + 1) * LANES)
            s = _dot_nt(qs[c], k_ref[bb, pl.ds(k0, nk), lanes])
            v = v_ref[bb, pl.ds(k0, nk), lanes]
            if first:
                p = _first_probs(s + cb_ref[(q_first - n_full * tk) // tq], m_ref.at[c])
                acc_ref[c] = _dot(p, v)
            else:
                p, alpha = _tile_probs(s, m_ref.at[c])
                acc_ref[c] = alpha * acc_ref[c] + _dot(p, v)

    tile(pl.multiple_of(n_full * tk, tk), tk, first=True)
    _causal_sweep(n_full, tk, tile)
    for bb in range(bps):
        for pr in range(hps // 2):
            c = bb * hps + 2 * pr
            pair = _pair_lanes(_normalised(acc_ref[c]), _normalised(acc_ref[c + 1]))
            o_ref[bb, :, pr * LANES:(pr + 1) * LANES] = pair.astype(o_ref.dtype)


def _mla_attention(q, k, v, tq=512, tk=512, hps=4, bps=2):
    b, t, _ = q.shape
    assert t % tk == 0 and tk % tq == 0 and MLA_HEADS % hps == 0 and hps % 2 == 0 and b % bps == 0
    spec_t = pl.BlockSpec((bps, t, hps * LANES), lambda bi, p, i: (bi, 0, p))
    cb = _causal_bias_table(tq, tk)
    return pl.pallas_call(
        functools.partial(_mla_body, tq=tq, tk=tk, hps=hps, bps=bps),
        grid=(b // bps, MLA_HEADS // hps, t // tq),
        in_specs=[pl.BlockSpec((bps, tq, hps * LANES), lambda bi, p, i: (bi, i, p)), spec_t, spec_t,
                  _const_spec(cb.shape)],
        out_specs=pl.BlockSpec((bps, tq, hps // 2 * LANES), lambda bi, p, i: (bi, i, p)),
        out_shape=jax.ShapeDtypeStruct((b, t, MLA_HEADS // 2 * LANES), MXU_DTYPE),
        scratch_shapes=[pltpu.VMEM((bps * hps, tq, LANES), F32), pltpu.VMEM((bps * hps, tq, LANES), F32)],
        compiler_params=_cparams(("parallel", "parallel", "arbitrary")), name="mla_attn",
    )(q, k, v, cb)


def _diff_body(q_ref, k_ref, v_ref, lam_ref, g_ref, cb_ref, o_ref, m_ref, acc_ref, *, tq, tk, lam_init, npair):
    q_first = pl.program_id(2) * tq
    n_full = q_first // tk
    tile_of = lambda j: slice(j * LANES, (j + 1) * LANES)
    grp = lax.broadcasted_iota(jnp.int32, (tq, LANES), 1) // DIFF_QK
    q4s = []
    for j in range(npair):
        q = q_ref[0, :, tile_of(j)]
        q4s.append(jnp.concatenate([jnp.where(grp == c, q, jnp.zeros_like(q)) for c in range(4)], axis=0))
    head_rows = [slice(0, 2 * tq), slice(2 * tq, 4 * tq)]

    def tile(k0, nk, first=False):
        for j in range(npair):
            s = _dot_nt(q4s[j], k_ref[0, pl.ds(k0, nk), tile_of(j)])
            if first:
                bias = jnp.concatenate([cb_ref[(q_first - n_full * tk) // tq]] * 4, axis=0)
                p = _first_probs(s + bias, m_ref.at[j])
            else:
                p, alpha = _tile_probs(s, m_ref.at[j])
            for hl, r in enumerate(head_rows):
                pv = _dot(p[r], v_ref[0, pl.ds(k0, nk), tile_of(2 * j + hl)])
                acc_ref[j, r, :] = pv if first else alpha[r] * acc_ref[j, r, :] + pv

    tile(pl.multiple_of(n_full * tk, tk), tk, first=True)
    _causal_sweep(n_full, tk, tile)
    lf = lam_ref[...]
    lam = (jnp.exp(jnp.sum(lf[0:1] * lf[1:2], keepdims=True))
           - jnp.exp(jnp.sum(lf[2:3] * lf[3:4], keepdims=True)) + lam_init)
    lane = lax.broadcasted_iota(jnp.int32, (tq, LANES), 1)
    for j in range(npair):
        a = _normalised(acc_ref[j])
        normed = []
        for hl in range(2):
            o = a[2 * hl * tq:(2 * hl + 1) * tq] - lam * a[(2 * hl + 1) * tq:(2 * hl + 2) * tq]
            ms = jnp.sum(jnp.where(lane < DIFF_V, o * o, 0.0), axis=-1, keepdims=True) * (1.0 / DIFF_V)
            normed.append(o * lax.rsqrt(ms + EPS))
        out = _pair_lanes(normed[0], normed[1]) * g_ref[...] * (1.0 - lam_init)
        o_ref[0, :, tile_of(j)] = out.astype(o_ref.dtype)


def _diff_attention(q, k, v, diff_lambda, norm_g, lam_init, tq=512, tk=512, npair=2):
    b, t, _ = q.shape
    pairs = DIFF_HEADS // 2
    assert t % tk == 0 and tk % tq == 0 and pairs % npair == 0
    lam_pad = jnp.pad(diff_lambda.astype(F32), ((0, 4), (0, LANES - DIFF_QK)))
    g2 = jnp.tile(norm_g.astype(F32), 2).reshape(1, LANES)
    spec_qo = pl.BlockSpec((1, tq, npair * LANES), lambda bi, p, i: (bi, i, p))
    cb = _causal_bias_table(tq, tk)
    return pl.pallas_call(
        functools.partial(_diff_body, tq=tq, tk=tk, lam_init=lam_init, npair=npair),
        grid=(b, pairs // npair, t // tq),
        in_specs=[spec_qo, pl.BlockSpec((1, t, npair * LANES), lambda bi, p, i: (bi, 0, p)),
                  pl.BlockSpec((1, t, npair * 2 * LANES), lambda bi, p, i: (bi, 0, p)),
                  _const_spec(lam_pad.shape), _const_spec(g2.shape), _const_spec(cb.shape)],
        out_specs=spec_qo,
        out_shape=jax.ShapeDtypeStruct((b, t, pairs * LANES), MXU_DTYPE),
        scratch_shapes=[pltpu.VMEM((npair, 4 * tq, LANES), F32), pltpu.VMEM((npair, 4 * tq, LANES), F32)],
        compiler_params=_cparams(("parallel", "parallel", "arbitrary")), name="diff_attn",
    )(q, k, v, lam_pad, g2, cb)


def _compress_body(xk_ref, xv_ref, w1_ref, w1o_ref, pos_ref, b1_ref, w2_ref, kc_ref, vc_ref):
    for kv, (x_ref, o_ref) in enumerate(((xk_ref, kc_ref), (xv_ref, vc_ref))):
        x = x_ref[0]
        posb = _dot(jnp.broadcast_to(pos_ref[kv], (8, pos_ref.shape[-1])), w1o_ref[kv])[0:1] + b1_ref[kv]
        for g in range(NSA_GROUPS):
            ab = _dot(x, w1_ref[kv, g])
            first, second = ab[:, :CMP_HIDDEN], ab[:, CMP_HIDDEN:]
            nxt = jnp.concatenate([second[1:], second[:1]], axis=0)
            hid = first + nxt + posb
            act = (hid * jax.nn.sigmoid(hid)).astype(MXU_DTYPE)
            out = _dot(act, w2_ref[kv]).astype(o_ref.dtype)
            if kv == 0:
                o_ref[0, :, g * LANES:(g + 1) * LANES] = out[:, :LANES]
            else:
                o_ref[0, :, 2 * g * LANES:2 * (g + 1) * LANES] = out


def _compress_weights(pos, w1, b1, w2):
    z = w1.shape[0]
    half = CMP_BLOCK // 2
    assert half == CMP_STRIDE
    w1r = w1.reshape(z, 2, 2, half, NSA_D, CMP_HIDDEN)
    both = jnp.concatenate([w1r[:, :, 0], w1r[:, :, 1]], axis=-1).astype(MXU_DTYPE)
    eye = jnp.eye(NSA_GROUPS, dtype=MXU_DTYPE)
    w1g = both[:, :, None, :, None, :, :] * eye[None, None, :, None, :, None, None]
    w1g = w1g.reshape(z, 2, NSA_GROUPS, half * NSA_GROUPS * NSA_D, 2 * CMP_HIDDEN)
    zeros = jnp.zeros(w2.shape[:-1] + (2 * (LANES - NSA_D),), w2.dtype)
    w2p = jnp.concatenate([w2, zeros, w2], axis=-1).astype(MXU_DTYPE)
    posf = pos.reshape(z, 2, 1, CMP_BLOCK * NSA_D).astype(MXU_DTYPE)
    return w1g, w1.astype(MXU_DTYPE), posf, b1.reshape(z, 2, 1, CMP_HIDDEN).astype(F32), w2p


def _compress(kct, vct, weights, layer):
    b, nch, _ = kct.shape
    xk, xv = kct, vct
    args = [xk, xv] + list(weights)
    x_spec = pl.BlockSpec((1, nch, xk.shape[-1]), lambda bi: (bi, 0, 0))
    o_spec = lambda w: pl.BlockSpec((1, nch, w), lambda bi: (bi, 0, 0))
    widths = (NSA_GROUPS * LANES, 2 * NSA_GROUPS * LANES)
    return pl.pallas_call(
        _compress_body, grid=(b,),
        in_specs=[x_spec, x_spec] + [_layer_spec(a, (layer,)) for a in args[2:]],
        out_specs=[o_spec(w) for w in widths],
        out_shape=[jax.ShapeDtypeStruct((b, nch, w), MXU_DTYPE) for w in widths],
        compiler_params=_cparams(("parallel",)), name="nsa_compress",
    )(*args)


def _sel_map_t(nch):
    r_sel, r_cmp = SEL_BLOCK // CMP_STRIDE, CMP_BLOCK // CMP_STRIDE
    nc = nch - 1
    mt = np.zeros((nch * CMP_STRIDE // SEL_BLOCK, nch), np.float32)
    for j in range(mt.shape[0]):
        for m in range(r_sel):
            for n in range(r_cmp):
                idx = j * r_sel - m - n
                if 0 <= idx < nc:
                    mt[j, idx] += 1.0
    return mt


def _split3(x):
    hi = x.astype(MXU_DTYPE)
    r1 = x - hi.astype(F32)
    mid = r1.astype(MXU_DTYPE)
    lo = (r1 - mid.astype(F32)).astype(MXU_DTYPE)
    return hi, mid, lo


def _nsa_body(q_ref, kc_ref, vc_ref, ks_ref, vs_ref, kw_ref, vw_ref, gt_ref, mt_ref, ge_ref, cb_ref, wb_ref,
              o_ref, m_ref, acc_ref, *, nq, tk, win_rows, ng):
    rows = HG * nq
    q_first = pl.program_id(2) * nq
    blk_first = q_first // SEL_BLOCK
    nb = mt_ref.shape[0]
    lane = lax.broadcasted_iota(jnp.int32, (rows, LANES), 1)
    qpos = q_first + lax.broadcasted_iota(jnp.int32, (nq, 1), 0)
    tpos = jnp.concatenate([qpos] * HG, axis=0)
    is_q = lane < NSA_D
    even, odd = slice(0, 2 * nq), slice(2 * nq, 4 * nq)
    jrow = lax.broadcasted_iota(jnp.int32, (nb, nq), 0)
    cblk = blk_first + lax.broadcasted_iota(jnp.int32, (nb, nq), 1) // SEL_BLOCK
    tiles = lambda g, w, j=0: slice((g * w + j) * LANES, (g * w + j + 1) * LANES)

    def pv(p, v_ref, g, r0, nr):
        return jnp.concatenate([_dot(p[even], v_ref[0, pl.ds(r0, nr), tiles(g, 2, 0)]),
                                _dot(p[odd], v_ref[0, pl.ds(r0, nr), tiles(g, 2, 1)])], axis=0)

    def before_rank(g):
        qe = q_ref[0, :, tiles(g, 2, 0)].astype(F32)
        qo = q_ref[0, :, tiles(g, 2, 1)].astype(F32)
        q4 = jnp.concatenate([qe, qo, pltpu.roll(qe, NSA_D, 1), pltpu.roll(qo, NSA_D, 1)], axis=0)
        q_plain = jnp.where(is_q, q4, 0.0).astype(MXU_DTYPE)

        w0 = pl.multiple_of(jnp.maximum(q_first - WINDOW, 0), nq)
        wbias = wb_ref[jnp.minimum(q_first // nq, WINDOW // nq)]
        s = _dot_nt(q_plain, kw_ref[0, pl.ds(w0, win_rows), tiles(g, 1)]) + jnp.concatenate([wbias] * HG, axis=0)
        e = jnp.exp(s - jnp.max(s, axis=-1, keepdims=True))
        acc_win = pv(e.astype(MXU_DTYPE), vw_ref, g, w0, win_rows)
        g_all = sum(_dot(part, ge_ref[...]) for part in _split3(gt_ref[0, :, tiles(g, 1)]))

        s = _dot_nt(q_plain, kc_ref[0, :, tiles(g, 1)])
        n_idx = lax.broadcasted_iota(jnp.int32, s.shape, 1)
        valid = n_idx * CMP_STRIDE + (CMP_BLOCK - 1) <= tpos
        sm = jnp.where(valid, s, NEG)
        e = jnp.where(valid, jnp.exp(sm - jnp.max(sm, axis=-1, keepdims=True)), 0.0)
        den = jnp.sum(e, axis=-1, keepdims=True)
        p = e * jnp.where(den > 0.0, 1.0 / den, 0.0)
        o_cmp = pv(p.astype(MXU_DTYPE), vc_ref, g, 0, vc_ref.shape[1])

        pg = p[0:nq]
        for hl in range(1, HG):
            pg = pg + p[hl * nq:(hl + 1) * nq]
        p_slc_t = sum(_dot_nt(mt_ref[...], part) for part in _split3(pg))
        forced = (jrow == 0) | (jrow == cblk) | (jrow == cblk - 1)
        score = jnp.where(jrow > cblk, -1.0, p_slc_t + jnp.where(forced, FORCE, 0.0))
        return dict(q4=q4, acc_win=acc_win, g_all=g_all, o_cmp=o_cmp, score=score)

    st = [before_rank(g) for g in range(ng)]
    sub, piece = 8, 16
    last_blk = blk_first + nq // SEL_BLOCK - 1
    jl = lax.broadcasted_iota(jnp.int32, (sub, nq), 0)

    n_cnt = piece // sub

    def count_piece(cnts, ib, jb):
        cnts = list(cnts)
        for g in range(ng):
            score = st[g]["score"]
            for i in range(ib * piece, (ib + 1) * piece):
                row = jnp.broadcast_to(score[i:i + 1], (sub, nq))
                for h in range(n_cnt):
                    j0 = jb * piece + h * sub
                    grp = score[j0:j0 + sub]
                    if j0 > i:
                        ahead = row >= grp
                    elif j0 + sub - 1 < i:
                        ahead = row > grp
                    else:
                        ahead = (row > grp) | ((row == grp) & (jl > i - j0))
                    cnts[g * n_cnt + h] = cnts[g * n_cnt + h] + jnp.where(ahead, 1.0, 0.0)
        return tuple(cnts)

    cnt = [[] for _ in range(ng)]
    for jb in range(nb // piece):
        cnts = tuple(jnp.zeros((sub, nq), F32) for _ in range(ng * n_cnt))
        for ib in range(nb // piece):
            needed = max(ib, jb) * piece <= last_blk
            cnts = lax.cond(needed, functools.partial(count_piece, ib=ib, jb=jb), lambda c: c, cnts)
        for g in range(ng):
            cnt[g].extend(cnts[g * n_cnt:(g + 1) * n_cnt])

    n_full = q_first // tk
    k_diag = pl.multiple_of(n_full * tk, tk)

    def after_rank(g):
        sel_t = (jnp.concatenate(cnt[g], axis=0) < float(SEL_TOPK)) & (jrow <= cblk)
        neg_t = jnp.where(sel_t, 0.0, NEG)
        bias = []
        for c in range(nq // LANES):
            pieces = [jnp.zeros((NSA_D, LANES), F32), neg_t[:, c * LANES:(c + 1) * LANES]]
            if LANES - NSA_D - nb:
                pieces.append(jnp.zeros((LANES - NSA_D - nb, LANES), F32))
            bias.append(jnp.concatenate(pieces, axis=0).T)
        bias = jnp.concatenate(bias, axis=0)
        q_aug = jnp.where(is_q, st[g]["q4"], jnp.concatenate([bias] * HG, axis=0)).astype(MXU_DTYPE)
        s = _dot_nt(q_aug, ks_ref[0, pl.ds(k_diag, tk), tiles(g, 1)])
        p = _first_probs(s + jnp.concatenate([cb_ref[(q_first - n_full * tk) // nq]] * HG, axis=0), m_ref.at[g])
        acc_ref[g] = pv(p, vs_ref, g, k_diag, tk)
        return q_aug

    q_augs = [after_rank(g) for g in range(ng)]

    def slc_tile(k0, nk):
        for g in range(ng):
            p, alpha = _tile_probs(_dot_nt(q_augs[g], ks_ref[0, pl.ds(k0, nk), tiles(g, 1)]), m_ref.at[g])
            acc_ref[g] = alpha * acc_ref[g] + pv(p, vs_ref, g, k0, nk)

    _causal_sweep(n_full, tk, slc_tile)

    first = lax.broadcasted_iota(jnp.int32, (nq, LANES), 1) < HEAD_V
    for g in range(ng):
        g_all, o_cmp = st[g]["g_all"], st[g]["o_cmp"]
        for pr in range(HG // 2):
            ev, od = slice(pr * nq, (pr + 1) * nq), slice((2 + pr) * nq, (3 + pr) * nq)
            gates = [g_all[:, (3 * pr + br) * LANES:(3 * pr + br + 1) * LANES] for br in range(3)]
            out = gates[0] * jnp.where(first, o_cmp[ev], o_cmp[od])
            for gate, acc in zip(gates[1:], (acc_ref[g], st[g]["acc_win"])):
                num = jnp.where(first, acc[ev], acc[od])
                den = pltpu.roll(jnp.where(first, acc[od], acc[ev]), HEAD_V, 1)
                out = out + gate * (num / den)
            o_ref[0, :, tiles(g, 2, pr)] = out.astype(o_ref.dtype)


def _gate_spread():
    e = np.zeros((LANES, (HG // 2) * 3 * LANES), np.float32)
    for hl in range(HG):
        for br in range(3):
            c0 = ((hl // 2) * 3 + br) * LANES + (hl % 2) * HEAD_V
            e[hl * 3 + br, c0:c0 + HEAD_V] = 1.0
    return e


def _window_bias_table(nq, win_rows):
    tabs = []
    for o in range(WINDOW // nq + 1):
        delta = (o * nq + np.arange(nq)[:, None]) - (max(o * nq - WINDOW, 0) + np.arange(win_rows)[None, :])
        tabs.append(np.where((delta >= 0) & (delta < WINDOW), 0.0, NEG))
    return jnp.asarray(np.stack(tabs), F32)


def _nsa_attention(q, kc, vc, ks, vs, kw, vw, gates, nq=256, tk=512, ng=2):
    b, t, _ = q.shape
    nch = kc.shape[1]
    win_rows = WINDOW + nq
    assert t % tk == 0 and tk % nq == 0 and nq % LANES == 0 and WINDOW % nq == 0 and t >= win_rows and HG == 4
    assert NSA_GROUPS % ng == 0
    mt = jnp.asarray(_sel_map_t(nch), MXU_DTYPE)
    nb = mt.shape[0]
    assert nb <= LANES - NSA_D
    consts = [mt, jnp.asarray(_gate_spread(), MXU_DTYPE), _causal_bias_table(nq, tk), _window_bias_table(nq, win_rows)]
    rows = HG * nq
    spec_q = pl.BlockSpec((1, nq, ng * 2 * LANES), lambda bi, g, i: (bi, i, g))
    spec_kc = pl.BlockSpec((1, nch, ng * LANES), lambda bi, g, i: (bi, 0, g))
    spec_vc = pl.BlockSpec((1, nch, ng * 2 * LANES), lambda bi, g, i: (bi, 0, g))
    spec_k = pl.BlockSpec((1, t, ng * LANES), lambda bi, g, i: (bi, 0, g))
    spec_v = pl.BlockSpec((1, t, ng * 2 * LANES), lambda bi, g, i: (bi, 0, g))
    spec_g = pl.BlockSpec((1, nq, ng * LANES), lambda bi, g, i: (bi, i, g))
    return pl.pallas_call(
        functools.partial(_nsa_body, nq=nq, tk=tk, win_rows=win_rows, ng=ng),
        grid=(b, NSA_GROUPS // ng, t // nq),
        in_specs=[spec_q, spec_kc, spec_vc, spec_k, spec_v, spec_k, spec_v, spec_g] + [_const_spec(c.shape) for c in consts],
        out_specs=spec_q,
        out_shape=jax.ShapeDtypeStruct((b, t, NSA_GROUPS * 2 * LANES), MXU_DTYPE),
        scratch_shapes=[pltpu.VMEM((ng, rows, LANES), F32), pltpu.VMEM((ng, rows, LANES), F32)],
        compiler_params=_cparams(("parallel", "parallel", "arbitrary")), name="nsa_attn",
    )(q, kc, vc, ks, vs, kw, vw, gates, *consts)


def kernel(x, ffn_norm_g, ffn_w_gate, ffn_w_up, ffn_w_down, mix_norm_g, w_in, mla_q_norm_g, mla_w_uq, mla_kv_norm_g, mla_w_ukv, diff_lambda, diff_norm_g, nsa_cmp_pos, nsa_cmp_w1, nsa_cmp_b1, nsa_cmp_w2, nsa_gate_b, w_out, final_norm_g):
    b, t, d = x.shape
    depth = w_in.shape[0]
    tab = _rope_tables(t)
    wg, wu, wd = (w.astype(MXU_DTYPE) for w in (ffn_w_gate, ffn_w_up, ffn_w_down))
    w_cat, wuq, wukv = _mix_weights(w_in, mla_w_uq, mla_w_ukv)
    cmp_w = _compress_weights(nsa_cmp_pos, nsa_cmp_w1, nsa_cmp_b1, nsa_cmp_w2)
    wo = w_out.astype(MXU_DTYPE)
    sh = lambda a: a.reshape(b, t, a.shape[-1])
    flat = lambda a: a.reshape(b * t, a.shape[-1])
    x2 = x.reshape(b * t, d)
    for l in range(depth):
        lam_init = 0.8 - 0.6 * math.exp(-0.3 * l)
        x2 = _ffn(x2, ffn_norm_g[l, 0], wg, wu, wd, (l, 0))
        (qm, km, vm, qd, kd, vd, qn, kct, vct, ks, vs, kw, vw, gt) = _mixproj(
            x2, t, mix_norm_g[l], w_cat, tab, mla_q_norm_g[l], wuq, mla_kv_norm_g[l], wukv, nsa_gate_b[l], l)
        o_mla = _mla_attention(sh(qm), sh(km), sh(vm))
        o_diff = _diff_attention(sh(qd), sh(kd), sh(vd), diff_lambda[l], diff_norm_g[l], lam_init)
        chunks = lambda a: a.reshape(b, t // CMP_STRIDE, a.shape[-1])
        kc, vc = _compress(chunks(kct), chunks(vct), cmp_w, l)
        o_nsa = _nsa_attention(sh(qn), kc, vc, sh(ks), sh(vs), sh(kw), sh(vw), sh(gt))
        x2 = _ffn(x2, ffn_norm_g[l, 1], wg, wu, wd, (l, 1),
                  mix=(flat(o_mla), flat(o_diff), flat(o_nsa), wo, l),
                  final_g=final_norm_g if l == depth - 1 else None)
    return x2.reshape(b, t, d)
```

```python
import functools
import math

import numpy as np
import jax
import jax.numpy as jnp
from jax import lax
from jax.experimental import pallas as pl
from jax.experimental.pallas import tpu as pltpu

F32 = jnp.float32
MXU_DTYPE = jnp.bfloat16

LANES = 128
VMEM_LIMIT = 56 * 1024 * 1024

MLA_HEADS, MLA_NOPE, MLA_ROPE, MLA_V = 4, 64, 32, 64
DIFF_HEADS, DIFF_QK, DIFF_V = 4, 32, 64
NSA_HEADS, NSA_GROUPS, NSA_D = 8, 2, 64
CMP_BLOCK, CMP_STRIDE, CMP_HIDDEN = 32, 16, 256
SEL_BLOCK, SEL_TOPK, WINDOW = 64, 16, 512
ROPE_THETA, EPS, NEG, FORCE = 10000.0, 1e-6, -1e30, 1e4
HG = NSA_HEADS // NSA_GROUPS
HEAD_V = 64
assert MLA_V == DIFF_V == NSA_D == HEAD_V

P_CQ, P_CKV, P_DV, P_VCT, P_VS, P_VW, P_NG, P_END = 0, 256, 384, 896, 1024, 1280, 1536, 1792
R_KR, R_DQ, R_DK, R_NQ, R_KCT, R_KS, R_KW, R_END = 0, 128, 384, 640, 1152, 1280, 1536, 1792
T_C32, T_S32, T_C64, T_S64, T_CQ, T_SQ, T_OH, T_END = 0, 128, 256, 384, 512, 640, 768, 896


def _cparams(sem):
    return pltpu.CompilerParams(dimension_semantics=sem, vmem_limit_bytes=VMEM_LIMIT)


def _const_spec(shape):
    n = len(shape)
    return pl.BlockSpec(shape, lambda *_: (0,) * n)


def _layer_spec(stacked, idx):
    n = len(idx)
    zeros = (0,) * (stacked.ndim - n)
    return pl.BlockSpec((None,) * n + stacked.shape[n:], lambda *_: tuple(idx) + zeros,
                        pipeline_mode=pl.Buffered(1))


def _rms(x, g):
    return x * lax.rsqrt(jnp.mean(x * x, axis=-1, keepdims=True) + EPS) * g


def _dot(a, b):
    return jnp.dot(a, b, preferred_element_type=F32)


def _dot_nt(a, b):
    return lax.dot_general(a, b, (((1,), (1,)), ((), ())), preferred_element_type=F32)


def _ffn_body(*refs, f_chunk, mixed, final):
    refs = list(refs)
    x_ref = refs.pop(0)
    x = x_ref[...]
    if mixed:
        o_refs, wo_ref = refs[:3], refs[3]
        refs = refs[4:]
        row = 0
        for o_in in o_refs:
            x = x + _dot(o_in[...], wo_ref[row:row + o_in.shape[1], :])
            row += o_in.shape[1]
    g_ref, wg_ref, wu_ref, wd_ref = refs[:4]
    gf_ref = refs[4] if final else None
    o_ref = refs[-1]
    h = _rms(x, g_ref[...]).astype(MXU_DTYPE)
    d_ff = wg_ref.shape[1]
    acc = jnp.zeros(x.shape, F32)
    for c in range(d_ff // f_chunk):
        sl = slice(c * f_chunk, (c + 1) * f_chunk)
        gate = _dot(h, wg_ref[:, sl])
        up = _dot(h, wu_ref[:, sl])
        act = (gate * jax.nn.sigmoid(gate) * up).astype(MXU_DTYPE)
        acc = acc + _dot(act, wd_ref[sl, :])
    y = x + 0.5 * acc
    if final:
        y = _rms(y, gf_ref[...])
    o_ref[...] = y


def _ffn(x2, g, wg, wu, wd, widx, mix=None, final_g=None, tm=1024, f_chunk=256):
    n, d = x2.shape
    d_ff = wg.shape[-1]
    assert n % tm == 0 and d_ff % f_chunk == 0
    final, mixed = final_g is not None, mix is not None
    row = lambda w: pl.BlockSpec((tm, w), lambda i: (i, 0))
    in_specs, args = [row(d)], [x2]
    if mixed:
        *outs, w_out, layer = mix
        in_specs += [row(o.shape[1]) for o in outs] + [_layer_spec(w_out, (layer,))]
        args += list(outs) + [w_out]
    in_specs += [_const_spec((1, d))] + [_layer_spec(w, widx) for w in (wg, wu, wd)]
    args += [g.reshape(1, d), wg, wu, wd]
    if final:
        in_specs.append(_const_spec((1, d)))
        args.append(final_g.reshape(1, d))
    name = "ffn" + ("_mix" if mixed else "") + ("_final" if final else "")
    return pl.pallas_call(
        functools.partial(_ffn_body, f_chunk=f_chunk, mixed=mixed, final=final),
        grid=(n // tm,), in_specs=in_specs, out_specs=row(d),
        out_shape=jax.ShapeDtypeStruct((n, d), F32),
        compiler_params=_cparams(("parallel",)), name=name,
    )(*args)


def _pad_groups(w, width, lead=0):
    g = w.reshape(w.shape[:-1] + (-1, width))
    g = jnp.pad(g, [(0, 0)] * (g.ndim - 1) + [(lead, LANES - width - lead)])
    return g.reshape(w.shape[:-1] + (-1,))


def _mix_weights(w_in, w_uq, w_ukv):
    o = np.cumsum([0, 256, 128, 32, 256, 256, 256, 512, 768, 24]).tolist()
    cq, ckv, kr, dq, dk, dv, nq, nkv, ng = [w_in[..., o[i]:o[i + 1]] for i in range(9)]
    nkv = nkv.reshape(nkv.shape[:-1] + (3, 2, NSA_GROUPS * NSA_D))
    kct, vct = nkv[..., 0, 0, :], nkv[..., 0, 1, :]
    ks, vs = nkv[..., 1, 0, :], nkv[..., 1, 1, :]
    kw, vw = nkv[..., 2, 0, :], nkv[..., 2, 1, :]
    w_cat = jnp.concatenate(
        [cq, ckv, _pad_groups(dv, HEAD_V), vct, _pad_groups(vs, HEAD_V), _pad_groups(vw, HEAD_V),
         _pad_groups(ng, HG * 3),
         _pad_groups(kr, MLA_ROPE, lead=MLA_NOPE), dq, dk, nq, kct, _pad_groups(ks, NSA_D), _pad_groups(kw, NSA_D)],
        axis=-1).astype(MXU_DTYPE)
    assert w_cat.shape[-1] == P_END + R_END
    wuq = _pad_groups(w_uq, MLA_NOPE + MLA_ROPE).astype(MXU_DTYPE)
    ukv = w_ukv.reshape(w_ukv.shape[:-1] + (MLA_HEADS, MLA_NOPE + MLA_V))
    flat = lambda a: a.reshape(a.shape[:-2] + (-1,))
    wukv = jnp.concatenate([_pad_groups(flat(ukv[..., :MLA_NOPE]), MLA_NOPE),
                            _pad_groups(flat(ukv[..., MLA_NOPE:]), MLA_V)], axis=-1).astype(MXU_DTYPE)
    return w_cat, wuq, wukv


def _rope_tables(t):
    def cs(dim):
        inv = ROPE_THETA ** (-jnp.arange(0, dim, 2, dtype=F32) / dim)
        ang = jnp.arange(t, dtype=F32)[:, None] * inv[None, :]
        c, s = jnp.cos(ang), jnp.sin(ang)
        return jnp.concatenate([c, c], axis=1), jnp.concatenate([-s, s], axis=1)

    c32, s32 = cs(32)
    c64, s64 = cs(64)
    one, zero = jnp.ones((t, MLA_NOPE), F32), jnp.zeros((t, MLA_NOPE), F32)
    z32 = jnp.zeros((t, LANES - MLA_NOPE - MLA_ROPE), F32)
    blk = jnp.arange(t)[:, None] // SEL_BLOCK
    onehot = (blk == jnp.arange(LANES)[None, :] - NSA_D).astype(F32)
    return jnp.concatenate([jnp.tile(c32, (1, 4)), jnp.tile(s32, (1, 4)), jnp.tile(c64, (1, 2)),
                            jnp.tile(s64, (1, 2)), jnp.concatenate([one, c32, z32], axis=1),
                            jnp.concatenate([zero, s32, z32], axis=1), onehot], axis=1)


def _mixproj_body(x_ref, g_ref, w_ref, tab_ref, qg_ref, wuq_ref, kvg_ref, wukv_ref, gb_ref,
                  qm_ref, km_ref, vm_ref, qd_ref, kd_ref, vd_ref, qn_ref, kct_ref, vct_ref,
                  ks_ref, vs_ref, kw_ref, vw_ref, gt_ref, chunk_ref):
    h = _rms(x_ref[...], g_ref[...]).astype(MXU_DTYPE)
    yp = _dot(h, w_ref[:, 0:P_END])
    yr = _dot(h, w_ref[:, P_END:P_END + R_END])
    tab = lambda o: tab_ref[:, o:o + LANES]
    lane = lax.broadcasted_iota(jnp.int32, (1, LANES), 1)
    ones_hi = (lane >= HEAD_V).astype(F32)

    def swap_halves(x, dim):
        half = dim // 2
        return jnp.where(lane % dim < half, pltpu.roll(x, LANES - half, 1), pltpu.roll(x, half, 1))

    def rope_of(x, dim, c, s):
        return x * tab(c) + swap_halves(x, dim) * tab(s)

    def rope(off, dim, c, s):
        return rope_of(yr[:, off:off + LANES], dim, c, s)

    def put(ref, j, val):
        ref[:, j * LANES:(j + 1) * LANES] = val.astype(ref.dtype)

    cqn = _rms(yp[:, P_CQ:P_CKV], qg_ref[...]).astype(MXU_DTYPE)
    qa = _dot(cqn, wuq_ref[...])
    q_scale = (MLA_NOPE + MLA_ROPE) ** -0.5
    nq_lanes = MLA_HEADS * LANES
    for hh in range(MLA_HEADS):
        put(qm_ref, hh, rope_of(qa[:, hh * LANES:(hh + 1) * LANES], MLA_ROPE, T_CQ, T_SQ) * q_scale)
    ckvn = _rms(yp[:, P_CKV:P_DV], kvg_ref[...]).astype(MXU_DTYPE)
    kv = _dot(ckvn, wukv_ref[...])
    k_rope = rope(R_KR, MLA_ROPE, T_C32, T_S32)
    for hh in range(MLA_HEADS):
        put(km_ref, hh, kv[:, hh * LANES:(hh + 1) * LANES] + k_rope)
        put(vm_ref, hh, kv[:, nq_lanes:][:, hh * LANES:(hh + 1) * LANES] + ones_hi)
    d_scale = DIFF_QK ** -0.5
    for j in range(2):
        put(qd_ref, j, rope(R_DQ + j * LANES, DIFF_QK, T_C32, T_S32) * d_scale)
        put(kd_ref, j, rope(R_DK + j * LANES, DIFF_QK, T_C32, T_S32))
    for hh in range(DIFF_HEADS):
        put(vd_ref, hh, yp[:, P_DV + hh * LANES:P_DV + (hh + 1) * LANES] + ones_hi)
    n_scale = NSA_D ** -0.5
    for j in range(4):
        put(qn_ref, j, rope(R_NQ + j * LANES, NSA_D, T_C64, T_S64) * n_scale)
    n_chunks = chunk_ref.shape[0] // CMP_STRIDE
    for ref, val in ((kct_ref, rope(R_KCT, NSA_D, T_C64, T_S64)), (vct_ref, yp[:, P_VCT:P_VS])):
        chunk_ref[...] = val
        for tok in range(CMP_STRIDE):
            put(ref, tok, chunk_ref[pl.ds(tok, n_chunks, stride=CMP_STRIDE), :])
    for j in range(NSA_GROUPS):
        put(ks_ref, j, rope(R_KS + j * LANES, NSA_D, T_C64, T_S64) + tab(T_OH))
        put(kw_ref, j, rope(R_KW + j * LANES, NSA_D, T_C64, T_S64))
        for ref, off in ((vs_ref, P_VS), (vw_ref, P_VW)):
            v1 = yp[:, off + j * LANES:off + (j + 1) * LANES] + ones_hi
            put(ref, 2 * j, v1)
            put(ref, 2 * j + 1, pltpu.roll(v1, HEAD_V, 1))
    gt_ref[...] = jax.nn.sigmoid(yp[:, P_NG:P_END] + gb_ref[...])


def _mixproj(x2, t, g, w_cat, tab, qg, wuq, kvg, wukv, gate_b, layer, tm=512):
    n, d = x2.shape
    assert t % tm == 0
    tpb = t // tm
    widths = [512, 512, 512, 256, 256, 512, 512, 128, 128, 256, 512, 256, 512]
    out_shape = [jax.ShapeDtypeStruct((n, w), MXU_DTYPE) for w in widths]
    for i in (7, 8):
        out_shape[i] = jax.ShapeDtypeStruct((n // CMP_STRIDE, CMP_STRIDE * widths[i]), MXU_DTYPE)
    out_shape.append(jax.ShapeDtypeStruct((n, 2 * LANES), F32))
    out_specs = [pl.BlockSpec((tm * s.shape[0] // n, s.shape[1]), lambda i: (i, 0)) for s in out_shape]
    gb = _pad_groups(gate_b.reshape(1, -1), HG * 3)
    in_specs = [pl.BlockSpec((tm, d), lambda i: (i, 0)), _const_spec((1, d)), _layer_spec(w_cat, (layer,)),
                pl.BlockSpec((tm, T_END), lambda i: (i % tpb, 0)),
                _const_spec((1, qg.shape[0])), _layer_spec(wuq, (layer,)),
                _const_spec((1, kvg.shape[0])), _layer_spec(wukv, (layer,)), _const_spec(gb.shape)]
    return pl.pallas_call(
        _mixproj_body, grid=(n // tm,), in_specs=in_specs, out_specs=out_specs, out_shape=out_shape,
        scratch_shapes=[pltpu.VMEM((tm, LANES), F32)],
        compiler_params=_cparams(("parallel",)), name="mixproj",
    )(x2, g.reshape(1, d), w_cat, tab, qg.reshape(1, -1), wuq, kvg.reshape(1, -1), wukv, gb)


def _first_probs(s, m_ref):
    m = jnp.broadcast_to(jnp.max(s, axis=-1, keepdims=True), m_ref.shape)
    m_ref[...] = m
    return jnp.exp(s - jnp.concatenate([m] * (s.shape[1] // LANES), axis=1)).astype(MXU_DTYPE)


def _tile_probs(s, m_ref):
    m_old = m_ref[...]
    m_new = jnp.maximum(m_old, jnp.max(s, axis=-1, keepdims=True))
    m_ref[...] = m_new
    p = jnp.exp(s - jnp.concatenate([m_new] * (s.shape[1] // LANES), axis=1))
    return p.astype(MXU_DTYPE), jnp.exp(m_old - m_new)


def _causal_sweep(n_full, tk, tile_fn):
    def wide(j, carry):
        tile_fn(pl.multiple_of(j * (2 * tk), 2 * tk), 2 * tk)
        return carry

    def single(j, carry):
        tile_fn(pl.multiple_of((n_full - 1) * tk, tk), tk)
        return carry

    lax.fori_loop(0, n_full // 2, wide, 0)
    lax.fori_loop(0, n_full % 2, single, 0)


def _normalised(acc):
    return acc / pltpu.roll(acc, HEAD_V, 1)


def _causal_bias_table(nq, tk):
    off = np.arange(tk // nq)[:, None, None] * nq
    row = np.arange(nq)[None, :, None]
    col = np.arange(tk)[None, None, :]
    return jnp.asarray(np.where(col <= off + row, 0.0, NEG), F32)


def _pair_lanes(o_even, o_odd):
    lane = lax.broadcasted_iota(jnp.int32, o_even.shape, 1)
    return jnp.where(lane < HEAD_V, o_even, pltpu.roll(o_odd, HEAD_V, 1))


def _mla_body(q_ref, k_ref, v_ref, cb_ref, o_ref, m_ref, acc_ref, *, tq, tk, hps, bps):
    q_first = pl.program_id(2) * tq
    n_full = q_first // tk
    chains = [(bb, hh) for bb in range(bps) for hh in range(hps)]
    qs = [q_ref[bb, :, hh * LANES:(hh + 1) * LANES] for bb, hh in chains]

    def tile(k0, nk, first=False):
        for c, (bb, hh) in enumerate(chains):
            lanes = slice(hh * LANES, (hh + 1) * LANES)
            s = _dot_nt(qs[c], k_ref[bb, pl.ds(k0, nk), lanes])
            v = v_ref[bb, pl.ds(k0, nk), lanes]
            if first:
                p = _first_probs(s + cb_ref[(q_first - n_full * tk) // tq], m_ref.at[c])
                acc_ref[c] = _dot(p, v)
            else:
                p, alpha = _tile_probs(s, m_ref.at[c])
                acc_ref[c] = alpha * acc_ref[c] + _dot(p, v)

    tile(pl.multiple_of(n_full * tk, tk), tk, first=True)
    _causal_sweep(n_full, tk, tile)
    for bb in range(bps):
        for pr in range(hps // 2):
            c = bb * hps + 2 * pr
            pair = _pair_lanes(_normalised(acc_ref[c]), _normalised(acc_ref[c + 1]))
            o_ref[bb, :, pr * LANES:(pr + 1) * LANES] = pair.astype(o_ref.dtype)


def _mla_attention(q, k, v, tq=512, tk=512, hps=4, bps=2):
    b, t, _ = q.shape
    assert t % tk == 0 and tk % tq == 0 and MLA_HEADS % hps == 0 and hps % 2 == 0 and b % bps == 0
    spec_t = pl.BlockSpec((bps, t, hps * LANES), lambda bi, p, i: (bi, 0, p))
    cb = _causal_bias_table(tq, tk)
    return pl.pallas_call(
        functools.partial(_mla_body, tq=tq, tk=tk, hps=hps, bps=bps),
        grid=(b // bps, MLA_HEADS // hps, t // tq),
        in_specs=[pl.BlockSpec((bps, tq, hps * LANES), lambda bi, p, i: (bi, i, p)), spec_t, spec_t,
                  _const_spec(cb.shape)],
        out_specs=pl.BlockSpec((bps, tq, hps // 2 * LANES), lambda bi, p, i: (bi, i, p)),
        out_shape=jax.ShapeDtypeStruct((b, t, MLA_HEADS // 2 * LANES), MXU_DTYPE),
        scratch_shapes=[pltpu.VMEM((bps * hps, tq, LANES), F32), pltpu.VMEM((bps * hps, tq, LANES), F32)],
        compiler_params=_cparams(("parallel", "parallel", "arbitrary")), name="mla_attn",
    )(q, k, v, cb)


def _diff_body(q_ref, k_ref, v_ref, lam_ref, g_ref, cb_ref, o_ref, m_ref, acc_ref, *, tq, tk, lam_init, npair):
    q_first = pl.program_id(2) * tq
    n_full = q_first // tk
    tile_of = lambda j: slice(j * LANES, (j + 1) * LANES)
    grp = lax.broadcasted_iota(jnp.int32, (tq, LANES), 1) // DIFF_QK
    q4s = []
    for j in range(npair):
        q = q_ref[0, :, tile_of(j)]
        q4s.append(jnp.concatenate([jnp.where(grp == c, q, jnp.zeros_like(q)) for c in range(4)], axis=0))
    head_rows = [slice(0, 2 * tq), slice(2 * tq, 4 * tq)]

    def tile(k0, nk, first=False):
        for j in range(npair):
            s = _dot_nt(q4s[j], k_ref[0, pl.ds(k0, nk), tile_of(j)])
            if first:
                bias = jnp.concatenate([cb_ref[(q_first - n_full * tk) // tq]] * 4, axis=0)
                p = _first_probs(s + bias, m_ref.at[j])
            else:
                p, alpha = _tile_probs(s, m_ref.at[j])
            for hl, r in enumerate(head_rows):
                pv = _dot(p[r], v_ref[0, pl.ds(k0, nk), tile_of(2 * j + hl)])
                acc_ref[j, r, :] = pv if first else alpha[r] * acc_ref[j, r, :] + pv

    tile(pl.multiple_of(n_full * tk, tk), tk, first=True)
    _causal_sweep(n_full, tk, tile)
    lf = lam_ref[...]
    lam = (jnp.exp(jnp.sum(lf[0:1] * lf[1:2], keepdims=True))
           - jnp.exp(jnp.sum(lf[2:3] * lf[3:4], keepdims=True)) + lam_init)
    lane = lax.broadcasted_iota(jnp.int32, (tq, LANES), 1)
    for j in range(npair):
        a = _normalised(acc_ref[j])
        normed = []
        for hl in range(2):
            o = a[2 * hl * tq:(2 * hl + 1) * tq] - lam * a[(2 * hl + 1) * tq:(2 * hl + 2) * tq]
            ms = jnp.sum(jnp.where(lane < DIFF_V, o * o, 0.0), axis=-1, keepdims=True) * (1.0 / DIFF_V)
            normed.append(o * lax.rsqrt(ms + EPS))
        out = _pair_lanes(normed[0], normed[1]) * g_ref[...] * (1.0 - lam_init)
        o_ref[0, :, tile_of(j)] = out.astype(o_ref.dtype)


def _diff_attention(q, k, v, diff_lambda, norm_g, lam_init, tq=512, tk=512, npair=2):
    b, t, _ = q.shape
    pairs = DIFF_HEADS // 2
    assert t % tk == 0 and tk % tq == 0 and pairs % npair == 0
    lam_pad = jnp.pad(diff_lambda.astype(F32), ((0, 4), (0, LANES - DIFF_QK)))
    g2 = jnp.tile(norm_g.astype(F32), 2).reshape(1, LANES)
    spec_qo = pl.BlockSpec((1, tq, npair * LANES), lambda bi, p, i: (bi, i, p))
    cb = _causal_bias_table(tq, tk)
    return pl.pallas_call(
        functools.partial(_diff_body, tq=tq, tk=tk, lam_init=lam_init, npair=npair),
        grid=(b, pairs // npair, t // tq),
        in_specs=[spec_qo, pl.BlockSpec((1, t, npair * LANES), lambda bi, p, i: (bi, 0, p)),
                  pl.BlockSpec((1, t, npair * 2 * LANES), lambda bi, p, i: (bi, 0, p)),
                  _const_spec(lam_pad.shape), _const_spec(g2.shape), _const_spec(cb.shape)],
        out_specs=spec_qo,
        out_shape=jax.ShapeDtypeStruct((b, t, pairs * LANES), MXU_DTYPE),
        scratch_shapes=[pltpu.VMEM((npair, 4 * tq, LANES), F32), pltpu.VMEM((npair, 4 * tq, LANES), F32)],
        compiler_params=_cparams(("parallel", "parallel", "arbitrary")), name="diff_attn",
    )(q, k, v, lam_pad, g2, cb)


def _compress_body(xk_ref, xv_ref, w1_ref, w1o_ref, pos_ref, b1_ref, w2_ref, kc_ref, vc_ref):
    for kv, (x_ref, o_ref) in enumerate(((xk_ref, kc_ref), (xv_ref, vc_ref))):
        x = x_ref[0]
        posb = _dot(jnp.broadcast_to(pos_ref[kv], (8, pos_ref.shape[-1])), w1o_ref[kv])[0:1] + b1_ref[kv]
        for g in range(NSA_GROUPS):
            ab = _dot(x, w1_ref[kv, g])
            first, second = ab[:, :CMP_HIDDEN], ab[:, CMP_HIDDEN:]
            nxt = jnp.concatenate([second[1:], second[:1]], axis=0)
            hid = first + nxt + posb
            act = (hid * jax.nn.sigmoid(hid)).astype(MXU_DTYPE)
            out = _dot(act, w2_ref[kv]).astype(o_ref.dtype)
            if kv == 0:
                o_ref[0, :, g * LANES:(g + 1) * LANES] = out[:, :LANES]
            else:
                o_ref[0, :, 2 * g * LANES:2 * (g + 1) * LANES] = out


def _compress_weights(pos, w1, b1, w2):
    z = w1.shape[0]
    half = CMP_BLOCK // 2
    assert half == CMP_STRIDE
    w1r = w1.reshape(z, 2, 2, half, NSA_D, CMP_HIDDEN)
    both = jnp.concatenate([w1r[:, :, 0], w1r[:, :, 1]], axis=-1).astype(MXU_DTYPE)
    eye = jnp.eye(NSA_GROUPS, dtype=MXU_DTYPE)
    w1g = both[:, :, None, :, None, :, :] * eye[None, None, :, None, :, None, None]
    w1g = w1g.reshape(z, 2, NSA_GROUPS, half * NSA_GROUPS * NSA_D, 2 * CMP_HIDDEN)
    zeros = jnp.zeros(w2.shape[:-1] + (2 * (LANES - NSA_D),), w2.dtype)
    w2p = jnp.concatenate([w2, zeros, w2], axis=-1).astype(MXU_DTYPE)
    posf = pos.reshape(z, 2, 1, CMP_BLOCK * NSA_D).astype(MXU_DTYPE)
    return w1g, w1.astype(MXU_DTYPE), posf, b1.reshape(z, 2, 1, CMP_HIDDEN).astype(F32), w2p


def _compress(kct, vct, weights, layer):
    b, nch, _ = kct.shape
    xk, xv = kct, vct
    args = [xk, xv] + list(weights)
    x_spec = pl.BlockSpec((1, nch, xk.shape[-1]), lambda bi: (bi, 0, 0))
    o_spec = lambda w: pl.BlockSpec((1, nch, w), lambda bi: (bi, 0, 0))
    widths = (NSA_GROUPS * LANES, 2 * NSA_GROUPS * LANES)
    return pl.pallas_call(
        _compress_body, grid=(b,),
        in_specs=[x_spec, x_spec] + [_layer_spec(a, (layer,)) for a in args[2:]],
        out_specs=[o_spec(w) for w in widths],
        out_shape=[jax.ShapeDtypeStruct((b, nch, w), MXU_DTYPE) for w in widths],
        compiler_params=_cparams(("parallel",)), name="nsa_compress",
    )(*args)


def _sel_map_t(nch):
    r_sel, r_cmp = SEL_BLOCK // CMP_STRIDE, CMP_BLOCK // CMP_STRIDE
    nc = nch - 1
    mt = np.zeros((nch * CMP_STRIDE // SEL_BLOCK, nch), np.float32)
    for j in range(mt.shape[0]):
        for m in range(r_sel):
            for n in range(r_cmp):
                idx = j * r_sel - m - n
                if 0 <= idx < nc:
                    mt[j, idx] += 1.0
    return mt


def _split3(x):
    hi = x.astype(MXU_DTYPE)
    r1 = x - hi.astype(F32)
    mid = r1.astype(MXU_DTYPE)
    lo = (r1 - mid.astype(F32)).astype(MXU_DTYPE)
    return hi, mid, lo


def _nsa_body(q_ref, kc_ref, vc_ref, ks_ref, vs_ref, kw_ref, vw_ref, gt_ref, mt_ref, ge_ref, cb_ref, wb_ref,
              o_ref, m_ref, acc_ref, *, nq, tk, win_rows, ng):
    rows = HG * nq
    q_first = pl.program_id(2) * nq
    blk_first = q_first // SEL_BLOCK
    nb = mt_ref.shape[0]
    lane = lax.broadcasted_iota(jnp.int32, (rows, LANES), 1)
    qpos = q_first + lax.broadcasted_iota(jnp.int32, (nq, 1), 0)
    tpos = jnp.concatenate([qpos] * HG, axis=0)
    is_q = lane < NSA_D
    even, odd = slice(0, 2 * nq), slice(2 * nq, 4 * nq)
    jrow = lax.broadcasted_iota(jnp.int32, (nb, nq), 0)
    cblk = blk_first + lax.broadcasted_iota(jnp.int32, (nb, nq), 1) // SEL_BLOCK
    tiles = lambda g, w, j=0: slice((g * w + j) * LANES, (g * w + j + 1) * LANES)

    def pv(p, v_ref, g, r0, nr):
        return jnp.concatenate([_dot(p[even], v_ref[0, pl.ds(r0, nr), tiles(g, 2, 0)]),
                                _dot(p[odd], v_ref[0, pl.ds(r0, nr), tiles(g, 2, 1)])], axis=0)

    def before_rank(g):
        qe = q_ref[0, :, tiles(g, 2, 0)].astype(F32)
        qo = q_ref[0, :, tiles(g, 2, 1)].astype(F32)
        q4 = jnp.concatenate([qe, qo, pltpu.roll(qe, NSA_D, 1), pltpu.roll(qo, NSA_D, 1)], axis=0)
        q_plain = jnp.where(is_q, q4, 0.0).astype(MXU_DTYPE)

        w0 = pl.multiple_of(jnp.maximum(q_first - WINDOW, 0), nq)
        wbias = wb_ref[jnp.minimum(q_first // nq, WINDOW // nq)]
        s = _dot_nt(q_plain, kw_ref[0, pl.ds(w0, win_rows), tiles(g, 1)]) + jnp.concatenate([wbias] * HG, axis=0)
        e = jnp.exp(s - jnp.max(s, axis=-1, keepdims=True))
        acc_win = pv(e.astype(MXU_DTYPE), vw_ref, g, w0, win_rows)
        g_all = sum(_dot(part, ge_ref[...]) for part in _split3(gt_ref[0, :, tiles(g, 1)]))

        s = _dot_nt(q_plain, kc_ref[0, :, tiles(g, 1)])
        n_idx = lax.broadcasted_iota(jnp.int32, s.shape, 1)
        valid = n_idx * CMP_STRIDE + (CMP_BLOCK - 1) <= tpos
        sm = jnp.where(valid, s, NEG)
        e = jnp.where(valid, jnp.exp(sm - jnp.max(sm, axis=-1, keepdims=True)), 0.0)
        den = jnp.sum(e, axis=-1, keepdims=True)
        p = e * jnp.where(den > 0.0, 1.0 / den, 0.0)
        o_cmp = pv(p.astype(MXU_DTYPE), vc_ref, g, 0, vc_ref.shape[1])

        pg = p[0:nq]
        for hl in range(1, HG):
            pg = pg + p[hl * nq:(hl + 1) * nq]
        p_slc_t = sum(_dot_nt(mt_ref[...], part) for part in _split3(pg))
        forced = (jrow == 0) | (jrow == cblk) | (jrow == cblk - 1)
        score = jnp.where(jrow > cblk, -1.0, p_slc_t + jnp.where(forced, FORCE, 0.0))
        return dict(q4=q4, acc_win=acc_win, g_all=g_all, o_cmp=o_cmp, score=score)

    st = [before_rank(g) for g in range(ng)]
    sub, piece = 8, 16
    last_blk = blk_first + nq // SEL_BLOCK - 1
    jl = lax.broadcasted_iota(jnp.int32, (sub, nq), 0)

    n_cnt = piece // sub

    def count_piece(cnts, ib, jb):
        cnts = list(cnts)
        for g in range(ng):
            score = st[g]["score"]
            for i in range(ib * piece, (ib + 1) * piece):
                row = jnp.broadcast_to(score[i:i + 1], (sub, nq))
                for h in range(n_cnt):
                    j0 = jb * piece + h * sub
                    grp = score[j0:j0 + sub]
                    if j0 > i:
                        ahead = row >= grp
                    elif j0 + sub - 1 < i:
                        ahead = row > grp
                    else:
                        ahead = (row > grp) | ((row == grp) & (jl > i - j0))
                    cnts[g * n_cnt + h] = cnts[g * n_cnt + h] + jnp.where(ahead, 1.0, 0.0)
        return tuple(cnts)

    cnt = [[] for _ in range(ng)]
    for jb in range(nb // piece):
        cnts = tuple(jnp.zeros((sub, nq), F32) for _ in range(ng * n_cnt))
        for ib in range(nb // piece):
            needed = max(ib, jb) * piece <= last_blk
            cnts = lax.cond(needed, functools.partial(count_piece, ib=ib, jb=jb), lambda c: c, cnts)
        for g in range(ng):
            cnt[g].extend(cnts[g * n_cnt:(g + 1) * n_cnt])

    n_full = q_first // tk
    k_diag = pl.multiple_of(n_full * tk, tk)

    def after_rank(g):
        sel_t = (jnp.concatenate(cnt[g], axis=0) < float(SEL_TOPK)) & (jrow <= cblk)
        neg_t = jnp.where(sel_t, 0.0, NEG)
        bias = []
        for c in range(nq // LANES):
            pieces = [jnp.zeros((NSA_D, LANES), F32), neg_t[:, c * LANES:(c + 1) * LANES]]
            if LANES - NSA_D - nb:
                pieces.append(jnp.zeros((LANES - NSA_D - nb, LANES), F32))
            bias.append(jnp.concatenate(pieces, axis=0).T)
        bias = jnp.concatenate(bias, axis=0)
        q_aug = jnp.where(is_q, st[g]["q4"], jnp.concatenate([bias] * HG, axis=0)).astype(MXU_DTYPE)
        s = _dot_nt(q_aug, ks_ref[0, pl.ds(k_diag, tk), tiles(g, 1)])
        p = _first_probs(s + jnp.concatenate([cb_ref[(q_first - n_full * tk) // nq]] * HG, axis=0), m_ref.at[g])
        acc_ref[g] = pv(p, vs_ref, g, k_diag, tk)
        return q_aug

    q_augs = [after_rank(g) for g in range(ng)]

    def slc_tile(k0, nk):
        for g in range(ng):
            p, alpha = _tile_probs(_dot_nt(q_augs[g], ks_ref[0, pl.ds(k0, nk), tiles(g, 1)]), m_ref.at[g])
            acc_ref[g] = alpha * acc_ref[g] + pv(p, vs_ref, g, k0, nk)

    _causal_sweep(n_full, tk, slc_tile)

    first = lax.broadcasted_iota(jnp.int32, (nq, LANES), 1) < HEAD_V
    for g in range(ng):
        g_all, o_cmp = st[g]["g_all"], st[g]["o_cmp"]
        for pr in range(HG // 2):
            ev, od = slice(pr * nq, (pr + 1) * nq), slice((2 + pr) * nq, (3 + pr) * nq)
            gates = [g_all[:, (3 * pr + br) * LANES:(3 * pr + br + 1) * LANES] for br in range(3)]
            out = gates[0] * jnp.where(first, o_cmp[ev], o_cmp[od])
            for gate, acc in zip(gates[1:], (acc_ref[g], st[g]["acc_win"])):
                num = jnp.where(first, acc[ev], acc[od])
                den = pltpu.roll(jnp.where(first, acc[od], acc[ev]), HEAD_V, 1)
                out = out + gate * (num / den)
            o_ref[0, :, tiles(g, 2, pr)] = out.astype(o_ref.dtype)


def _gate_spread():
    e = np.zeros((LANES, (HG // 2) * 3 * LANES), np.float32)
    for hl in range(HG):
        for br in range(3):
            c0 = ((hl // 2) * 3 + br) * LANES + (hl % 2) * HEAD_V
            e[hl * 3 + br, c0:c0 + HEAD_V] = 1.0
    return e


def _window_bias_table(nq, win_rows):
    tabs = []
    for o in range(WINDOW // nq + 1):
        delta = (o * nq + np.arange(nq)[:, None]) - (max(o * nq - WINDOW, 0) + np.arange(win_rows)[None, :])
        tabs.append(np.where((delta >= 0) & (delta < WINDOW), 0.0, NEG))
    return jnp.asarray(np.stack(tabs), F32)


def _nsa_attention(q, kc, vc, ks, vs, kw, vw, gates, nq=256, tk=512, ng=2):
    b, t, _ = q.shape
    nch = kc.shape[1]
    win_rows = WINDOW + nq
    assert t % tk == 0 and tk % nq == 0 and nq % LANES == 0 and WINDOW % nq == 0 and t >= win_rows and HG == 4
    assert NSA_GROUPS % ng == 0
    mt = jnp.asarray(_sel_map_t(nch), MXU_DTYPE)
    nb = mt.shape[0]
    assert nb <= LANES - NSA_D
    consts = [mt, jnp.asarray(_gate_spread(), MXU_DTYPE), _causal_bias_table(nq, tk), _window_bias_table(nq, win_rows)]
    rows = HG * nq
    spec_q = pl.BlockSpec((1, nq, ng * 2 * LANES), lambda bi, g, i: (bi, i, g))
    spec_kc = pl.BlockSpec((1, nch, ng * LANES), lambda bi, g, i: (bi, 0, g))
    spec_vc = pl.BlockSpec((1, nch, ng * 2 * LANES), lambda bi, g, i: (bi, 0, g))
    spec_k = pl.BlockSpec((1, t, ng * LANES), lambda bi, g, i: (bi, 0, g))
    spec_v = pl.BlockSpec((1, t, ng * 2 * LANES), lambda bi, g, i: (bi, 0, g))
    spec_g = pl.BlockSpec((1, nq, ng * LANES), lambda bi, g, i: (bi, i, g))
    return pl.pallas_call(
        functools.partial(_nsa_body, nq=nq, tk=tk, win_rows=win_rows, ng=ng),
        grid=(b, NSA_GROUPS // ng, t // nq),
        in_specs=[spec_q, spec_kc, spec_vc, spec_k, spec_v, spec_k, spec_v, spec_g] + [_const_spec(c.shape) for c in consts],
        out_specs=spec_q,
        out_shape=jax.ShapeDtypeStruct((b, t, NSA_GROUPS * 2 * LANES), MXU_DTYPE),
        scratch_shapes=[pltpu.VMEM((ng, rows, LANES), F32), pltpu.VMEM((ng, rows, LANES), F32)],
        compiler_params=_cparams(("parallel", "parallel", "arbitrary")), name="nsa_attn",
    )(q, kc, vc, ks, vs, kw, vw, gates, *consts)


def kernel(x, ffn_norm_g, ffn_w_gate, ffn_w_up, ffn_w_down, mix_norm_g, w_in, mla_q_norm_g, mla_w_uq, mla_kv_norm_g, mla_w_ukv, diff_lambda, diff_norm_g, nsa_cmp_pos, nsa_cmp_w1, nsa_cmp_b1, nsa_cmp_w2, nsa_gate_b, w_out, final_norm_g):
    b, t, d = x.shape
    depth = w_in.shape[0]
    tab = _rope_tables(t)
    wg, wu, wd = (w.astype(MXU_DTYPE) for w in (ffn_w_gate, ffn_w_up, ffn_w_down))
    w_cat, wuq, wukv = _mix_weights(w_in, mla_w_uq, mla_w_ukv)
    cmp_w = _compress_weights(nsa_cmp_pos, nsa_cmp_w1, nsa_cmp_b1, nsa_cmp_w2)
    wo = w_out.astype(MXU_DTYPE)
    sh = lambda a: a.reshape(b, t, a.shape[-1])
    flat = lambda a: a.reshape(b * t, a.shape[-1])
    x2 = x.reshape(b * t, d)
    for l in range(depth):
        lam_init = 0.8 - 0.6 * math.exp(-0.3 * l)
        x2 = _ffn(x2, ffn_norm_g[l, 0], wg, wu, wd, (l, 0))
        (qm, km, vm, qd, kd, vd, qn, kct, vct, ks, vs, kw, vw, gt) = _mixproj(
            x2, t, mix_norm_g[l], w_cat, tab, mla_q_norm_g[l], wuq, mla_kv_norm_g[l], wukv, nsa_gate_b[l], l)
        o_mla = _mla_attention(sh(qm), sh(km), sh(vm))
        o_diff = _diff_attention(sh(qd), sh(kd), sh(vd), diff_lambda[l], diff_norm_g[l], lam_init)
        chunks = lambda a: a.reshape(b, t // CMP_STRIDE, a.shape[-1])
        kc, vc = _compress(chunks(kct), chunks(vct), cmp_w, l)
        o_nsa = _nsa_attention(sh(qn), kc, vc, sh(ks), sh(vs), sh(kw), sh(vw), sh(gt))
        x2 = _ffn(x2, ffn_norm_g[l, 1], wg, wu, wd, (l, 1),
                  mix=(flat(o_mla), flat(o_diff), flat(o_nsa), wo, l),
                  final_g=final_norm_g if l == depth - 1 else None)
    return x2.reshape(b, t, d)
```
